```python
import jax, jax.numpy as jnp
from jax import lax
import numpy as np

D_MODEL = 1024
BATCH = 32
SEQ = 2048
DEPTH = 2
DEC_BATCH = 8
DEC_SEQ = 4096
PAST_LEN = 128

A_HEADS = 4
A_V = D_MODEL // A_HEADS
A_QK = A_V // 2
B_HEADS = 4
B_V = D_MODEL // B_HEADS
B_QK = D_MODEL // B_HEADS
D_FF = ((8 * D_MODEL // 3 + 127) // 128) * 128
CONV_W = 3
CHUNK = 128
ALPHA = (2 * DEPTH) ** 0.25
BETA = (8 * DEPTH) ** -0.25
LN_EPS = 1e-5
ROPE_BASE = 10000.0

A_QK_W = A_HEADS * A_QK
A_V_W = A_HEADS * A_V
A_GATE_W = 2 * 2 * A_HEADS
B_QK_W = B_HEADS * B_QK
B_V_W = B_HEADS * B_V
SPLITS = (A_QK_W, A_QK_W, A_V_W, A_V_W, A_GATE_W, B_QK_W, B_QK_W, B_V_W, B_V_W, D_MODEL, D_MODEL)
N_IN = sum(SPLITS)

kernel_name = "hybrid_mlstm_retention_encoder"


def layer_norm(x, g, b):
    xf = x.astype(jnp.float32)
    mu = jnp.mean(xf, -1, keepdims=True)
    var = jnp.mean(jnp.square(xf - mu), -1, keepdims=True)
    return ((xf - mu) * lax.rsqrt(var + LN_EPS)).astype(x.dtype) * g + b


def head_norm(h, g, out_dtype):
    hf = h.astype(jnp.float32)
    mu = jnp.mean(hf, -1, keepdims=True)
    var = jnp.mean(jnp.square(hf - mu), -1, keepdims=True)
    y = (hf - mu) * lax.rsqrt(var + LN_EPS)
    return y.reshape(h.shape[0], h.shape[1], -1).astype(out_dtype) * g


def split_cols(u):
    idx = [int(v) for v in np.cumsum(SPLITS)[:-1]]
    return jnp.split(u, idx, axis=-1)


def to_heads(a, n_heads):
    b, s, _ = a.shape
    return a.reshape(b, s, n_heads, -1).transpose(0, 2, 1, 3)


def rope(t, cos, sin):
    c = cos[:, None, :]
    s = sin[:, None, :]
    t1, t2 = jnp.split(t, 2, axis=-1)
    return jnp.concatenate([t1 * c - t2 * s, t1 * s + t2 * c], axis=-1)


def to_chunks(a):
    b, h, s = a.shape[:3]
    return jnp.moveaxis(a.reshape(b, h, s // CHUNK, CHUNK, *a.shape[3:]), 2, 0)


def from_chunks(a):
    n, b, h, l, d = a.shape
    return jnp.moveaxis(a, 0, 2).reshape(b, h, n * l, d)


def mlstm_chunkwise(q, k, v, i_pre, logf):
    b, h, _, dk = q.shape
    dv = v.shape[-1]
    tri = jnp.tril(jnp.ones((CHUNK, CHUNK), dtype=bool))

    def step(carry, xs):
        C, nv, m = carry
        qb, kb, vb, ib, fb = xs
        cb = jnp.cumsum(fb, axis=-1)
        dmat = jnp.where(tri, cb[..., :, None] - cb[..., None, :] + ib[..., None, :], -jnp.inf)
        inter = cb + m[..., None]
        m_t = jnp.maximum(inter, jnp.max(dmat, axis=-1))
        w_inter = jnp.exp(inter - m_t)
        p = jnp.einsum('bhtd,bhsd->bhts', qb, kb) * jnp.exp(dmat - m_t[..., None])
        num = jnp.einsum('bhts,bhsv->bhtv', p, vb) + w_inter[..., None] * jnp.einsum('bhtd,bhdv->bhtv', qb, C)
        den = jnp.sum(p, axis=-1) + w_inter * jnp.einsum('bhtd,bhd->bht', qb, nv)
        h_out = num / jnp.maximum(jnp.abs(den), jnp.exp(-m_t))[..., None]
        c_last = cb[..., -1]
        g_s = c_last[..., None] - cb + ib
        m_new = jnp.maximum(c_last + m, jnp.max(g_s, axis=-1))
        w_s = jnp.exp(g_s - m_new[..., None])
        dec = jnp.exp(c_last + m - m_new)
        kw = kb * w_s[..., None]
        C = dec[..., None, None] * C + jnp.einsum('bhsd,bhsv->bhdv', kw, vb)
        nv = dec[..., None] * nv + jnp.sum(kw, axis=2)
        return (C, nv, m_new), h_out

    init = (jnp.zeros((b, h, dk, dv), jnp.float32), jnp.zeros((b, h, dk), jnp.float32),
            jnp.zeros((b, h), jnp.float32))
    _, hs = lax.scan(step, init, (to_chunks(q), to_chunks(k), to_chunks(v), to_chunks(i_pre), to_chunks(logf)))
    return from_chunks(hs)


def retention_chunkwise(q, k, v, lg):
    b, h, _, dk = q.shape
    dv = v.shape[-1]
    pos = jnp.arange(CHUNK, dtype=jnp.float32)
    diff = pos[:, None] - pos[None, :]
    dmat = jnp.where(diff >= 0, jnp.exp(lg[:, None, None] * jnp.maximum(diff, 0.0)), 0.0)
    w_inter = jnp.exp(lg[:, None] * (pos + 1.0))
    w_state = jnp.exp(lg[:, None] * (CHUNK - 1.0 - pos))
    chunk_dec = jnp.exp(lg * CHUNK)

    def step(R, xs):
        qb, kb, vb = xs
        a = jnp.einsum('bhtd,bhsd->bhts', qb, kb) * dmat
        o = jnp.einsum('bhts,bhsv->bhtv', a, vb) + w_inter[..., None] * jnp.einsum('bhtd,bhdv->bhtv', qb, R)
        R = chunk_dec[:, None, None] * R + jnp.einsum('bhsd,bhsv->bhdv', kb * w_state[..., None], vb)
        return R, o

    init = jnp.zeros((b, h, dk, dv), jnp.float32)
    _, os_ = lax.scan(step, init, (to_chunks(q), to_chunks(k), to_chunks(v)))
    return from_chunks(os_)


def flip_s(a):
    return jnp.flip(a, axis=2)


def encoder_layer(x, cos, sin, w_in, b_if, lg, g_a, g_b, w_o, ln1_g, ln1_b,
                  w_up, conv_w, conv_b, w_down, ln2_g, ln2_b):
    bsz, s, _ = x.shape
    dt = x.dtype
    u = jnp.einsum('bsd,dn->bsn', x, w_in)
    qa, ka, va, oa, gif, qb, kb, vb, gb, m_a, m_b = split_cols(u)

    qa = to_heads(qa, A_HEADS) * (A_QK ** -0.5)
    ka = to_heads(ka, A_HEADS)
    va = to_heads(va, A_HEADS)
    gif = gif.astype(jnp.float32).reshape(bsz, s, 2, 2, A_HEADS) + b_if.astype(jnp.float32)
    i_pre = jnp.transpose(gif[:, :, :, 0, :], (2, 0, 3, 1))
    logf = jnp.transpose(jax.nn.log_sigmoid(gif[:, :, :, 1, :]), (2, 0, 3, 1))
    h_f = mlstm_chunkwise(qa, ka, va, i_pre[0], logf[0])
    h_bw = flip_s(mlstm_chunkwise(flip_s(qa), flip_s(ka), flip_s(va), flip_s(i_pre[1]), flip_s(logf[1])))
    h_a = (h_f + h_bw).transpose(0, 2, 1, 3)
    h_a = h_a * jax.nn.sigmoid(oa.astype(jnp.float32)).reshape(bsz, s, A_HEADS, A_V)
    y_a = head_norm(h_a, g_a, dt)

    qb = rope(qb.reshape(bsz, s, B_HEADS, B_QK), cos, sin).transpose(0, 2, 1, 3)
    kb = (rope(kb.reshape(bsz, s, B_HEADS, B_QK), cos, sin) * (B_QK ** -0.5)).transpose(0, 2, 1, 3)
    vb = to_heads(vb, B_HEADS)
    lg = lg.astype(jnp.float32)
    r_f = retention_chunkwise(qb, kb, vb, lg[0])
    r_bw = flip_s(retention_chunkwise(flip_s(qb), flip_s(kb), flip_s(vb), lg[1]))
    y_b = head_norm((r_f + r_bw).transpose(0, 2, 1, 3), g_b, dt) * jax.nn.silu(gb)

    merged = jax.nn.sigmoid(m_a) * y_a + jax.nn.sigmoid(m_b) * y_b
    x = layer_norm(ALPHA * x + jnp.einsum('bsd,de->bse', merged, w_o), ln1_g, ln1_b)

    hu = jnp.einsum('bsd,df->bsf', x, w_up)
    hp = jnp.pad(hu, ((0, 0), (1, 1), (0, 0)))
    hc = conv_w[0] * hp[:, :-2] + conv_w[1] * hp[:, 1:-1] + conv_w[2] * hp[:, 2:] + conv_b
    val, gate = jnp.split(hc, 2, axis=-1)
    f = jax.nn.gelu(gate) * val
    x = layer_norm(ALPHA * x + jnp.einsum('bsf,fd->bsd', f, w_down), ln2_g, ln2_b)
    return x


def trunk(x, ln_in_g, ln_in_b, w_in, b_if, ret_log_decay, mlstm_norm_g, ret_norm_g, w_o,
          ln1_g, ln1_b, w_up, conv_w, conv_b, w_down, ln2_g, ln2_b):
    s = x.shape[1]
    inv_freq = ROPE_BASE ** (-jnp.arange(0, B_QK, 2, dtype=jnp.float32) / B_QK)
    ang = jnp.arange(s, dtype=jnp.float32)[:, None] * inv_freq[None, :]
    cos = jnp.cos(ang).astype(x.dtype)
    sin = jnp.sin(ang).astype(x.dtype)
    x = layer_norm(x, ln_in_g, ln_in_b)
    for l in range(DEPTH):
        x = encoder_layer(x, cos, sin, w_in[l], b_if[l], ret_log_decay[l], mlstm_norm_g[l], ret_norm_g[l],
                          w_o[l], ln1_g[l], ln1_b[l], w_up[l], conv_w[l], conv_b[l], w_down[l],
                          ln2_g[l], ln2_b[l])
    return x


def setup_inputs(seed: int = 0) -> dict:
    key = jax.random.key(seed)
    ks = jax.random.split(key, 24)
    f32 = jnp.float32

    def nrm(k, shape, scale):
        return jax.random.normal(k, shape, f32) * scale

    x_prompt = nrm(ks[0], (BATCH, SEQ, D_MODEL), 1.0)
    x_sample = nrm(ks[1], (DEC_BATCH, DEC_SEQ, D_MODEL), 1.0)
    ln_in_g = 1.0 + nrm(ks[2], (D_MODEL,), 0.02)
    ln_in_b = nrm(ks[3], (D_MODEL,), 0.02)
    w_in = nrm(ks[4], (DEPTH, D_MODEL, N_IN), D_MODEL ** -0.5)
    i_bias = nrm(ks[5], (DEPTH, 2, A_HEADS), 0.1)
    f_bias = jnp.linspace(3.0, 6.0, A_HEADS, dtype=f32) + nrm(ks[6], (DEPTH, 2, A_HEADS), 0.1)
    b_if = jnp.stack([i_bias, f_bias], axis=2)
    base = jnp.log1p(-(2.0 ** (-5.0 - jnp.arange(B_HEADS, dtype=f32))))
    ret_log_decay = base * jnp.exp(nrm(ks[7], (DEPTH, 2, B_HEADS), 0.1))
    mlstm_norm_g = 1.0 + nrm(ks[8], (DEPTH, A_V_W), 0.02)
    ret_norm_g = 1.0 + nrm(ks[9], (DEPTH, B_V_W), 0.02)
    w_o = nrm(ks[10], (DEPTH, D_MODEL, D_MODEL), BETA * D_MODEL ** -0.5)
    ln1_g = 1.0 + nrm(ks[11], (DEPTH, D_MODEL), 0.02)
    ln1_b = nrm(ks[12], (DEPTH, D_MODEL), 0.02)
    w_up = nrm(ks[13], (DEPTH, D_MODEL, 2 * D_FF), D_MODEL ** -0.5)
    conv_w = nrm(ks[14], (DEPTH, CONV_W, 2 * D_FF), CONV_W ** -0.5)
    conv_b = nrm(ks[15], (DEPTH, 2 * D_FF), 0.02)
    w_down = nrm(ks[16], (DEPTH, D_FF, D_MODEL), BETA * D_FF ** -0.5)
    ln2_g = 1.0 + nrm(ks[17], (DEPTH, D_MODEL), 0.02)
    ln2_b = nrm(ks[18], (DEPTH, D_MODEL), 0.02)
    return {"x_prompt": x_prompt, "x_sample": x_sample, "ln_in_g": ln_in_g, "ln_in_b": ln_in_b,
            "w_in": w_in, "b_if": b_if, "ret_log_decay": ret_log_decay, "mlstm_norm_g": mlstm_norm_g,
            "ret_norm_g": ret_norm_g, "w_o": w_o, "ln1_g": ln1_g, "ln1_b": ln1_b, "w_up": w_up,
            "conv_w": conv_w, "conv_b": conv_b, "w_down": w_down, "ln2_g": ln2_g, "ln2_b": ln2_b}


def reference(x_prompt, x_sample, ln_in_g, ln_in_b, w_in, b_if, ret_log_decay, mlstm_norm_g, ret_norm_g,
              w_o, ln1_g, ln1_b, w_up, conv_w, conv_b, w_down, ln2_g, ln2_b):
    y_prompt = trunk(x_prompt, ln_in_g, ln_in_b, w_in, b_if, ret_log_decay, mlstm_norm_g, ret_norm_g, w_o,
                     ln1_g, ln1_b, w_up, conv_w, conv_b, w_down, ln2_g, ln2_b)
    y_sample = trunk(x_sample, ln_in_g, ln_in_b, w_in, b_if, ret_log_decay, mlstm_norm_g, ret_norm_g, w_o,
                     ln1_g, ln1_b, w_up, conv_w, conv_b, w_down, ln2_g, ln2_b)
    return (y_prompt, y_sample)
```

```python
import functools

import jax
import jax.numpy as jnp
from jax import lax
from jax.experimental import pallas as pl
from jax.experimental.pallas import tpu as pltpu

F32 = jnp.float32
BF16 = jnp.bfloat16

D_MODEL = 1024
HEADS = 4
A_QK = 128
HEAD_V = 256
B_QK = 256
D_FF = 2816
CHUNK = 128
LN_EPS = 1e-5
ROPE_BASE = 10000.0

A_QK_W = HEADS * A_QK
V_W = HEADS * HEAD_V
B_QK_W = HEADS * B_QK
N_GATES = 16
KT_ROWS = A_QK_W + B_QK_W

_SPLITS = (A_QK_W, A_QK_W, V_W, V_W, N_GATES, B_QK_W, B_QK_W, V_W, V_W, D_MODEL, D_MODEL)
_OFF = [0]
for _w in _SPLITS:
    _OFF.append(_OFF[-1] + _w)
(O_QA, O_KA, O_VA, O_OA, O_G, O_QB, O_KB, O_VB, O_GB, O_MA, O_MB, _) = _OFF

PROJ_TM = 512
MERGE_TM = 256
FFN_TM = 512
FFN_TF = 256
HALO = 8
VMEM_LIMIT = 56 * 1024 * 1024

_NT = (((1,), (1,)), ((), ()))


def _dot(a, b):
    return jnp.dot(a, b, preferred_element_type=F32)


def _dot_nt(a, b):
    return lax.dot_general(a, b, _NT, preferred_element_type=F32)


def _layer_norm(x, g, b):
    mu = jnp.mean(x, axis=-1, keepdims=True)
    xc = x - mu
    var = jnp.mean(xc * xc, axis=-1, keepdims=True)
    return xc * lax.rsqrt(var + LN_EPS) * g + b


def _resident(shape):
    nd = len(shape)
    return pl.BlockSpec(shape, lambda *_: (0,) * nd, pipeline_mode=pl.Buffered(1))


def _proj_kernel(*refs, apply_ln):
    if apply_ln:
        (x_ref, lng_ref, lnb_ref, wtok_ref, wt_ref, wg_ref, bg_ref, cos_ref, sin_ref, cost_ref, sint_ref,
         qa_ref, va_ref, qb_ref, vb_ref, kt_ref, gt_ref, xn_ref) = refs
        x = _layer_norm(x_ref[...], lng_ref[...], lnb_ref[...])
        xn_ref[...] = x
    else:
        (x_ref, wtok_ref, wt_ref, wg_ref, bg_ref, cos_ref, sin_ref, cost_ref, sint_ref,
         qa_ref, va_ref, qb_ref, vb_ref, kt_ref, gt_ref) = refs
        x = x_ref[...]
    xb = x.astype(BF16)

    qa_ref[...] = (_dot(xb, wtok_ref[:, 0:A_QK_W]) * (A_QK ** -0.5)).astype(BF16)
    half = V_W // 2
    for j in range(2):
        c0 = A_QK_W + j * half
        va_ref[:, j * half:(j + 1) * half] = _dot(xb, wtok_ref[:, c0:c0 + half]).astype(BF16)
    cos = cos_ref[...]
    sin = sin_ref[...]
    hq = B_QK // 2
    for h in range(HEADS):
        c0 = A_QK_W + V_W + h * B_QK
        t = _dot(xb, wtok_ref[:, c0:c0 + B_QK])
        t1 = t[:, :hq]
        t2 = t[:, hq:]
        qb_ref[:, h * B_QK:h * B_QK + hq] = (t1 * cos - t2 * sin).astype(BF16)
        qb_ref[:, h * B_QK + hq:(h + 1) * B_QK] = (t1 * sin + t2 * cos).astype(BF16)
    for j in range(2):
        c0 = A_QK_W + V_W + B_QK_W + j * half
        vb_ref[:, j * half:(j + 1) * half] = _dot(xb, wtok_ref[:, c0:c0 + half]).astype(BF16)

    for j in range(2):
        r0 = j * 256
        kt_ref[r0:r0 + 256, :] = _dot_nt(wt_ref[r0:r0 + 256, :], xb).astype(BF16)
    cost = cost_ref[...]
    sint = sint_ref[...]
    for h in range(HEADS):
        r0 = A_QK_W + h * B_QK
        t = _dot_nt(wt_ref[r0:r0 + B_QK, :], xb)
        t1 = t[:hq, :]
        t2 = t[hq:, :]
        kt_ref[r0:r0 + hq, :] = ((t1 * cost - t2 * sint) * (B_QK ** -0.5)).astype(BF16)
        kt_ref[r0 + hq:r0 + B_QK, :] = ((t1 * sint + t2 * cost) * (B_QK ** -0.5)).astype(BF16)

    g = _dot_nt(wg_ref[...], xb) + bg_ref[...]
    row = lax.broadcasted_iota(jnp.int32, g.shape, 0)
    log_sig = jnp.minimum(g, 0.0) - jnp.log1p(jnp.exp(-jnp.abs(g)))
    gt_ref[...] = jnp.where((row & HEADS) != 0, log_sig, g)


def _proj_call(x2d, batch, seq, ln, wtok, wt, wg, bg, rope):
    tokens = x2d.shape[0]
    tm = PROJ_TM
    nt = seq // tm
    cos, sin, cost, sint = rope
    tok = lambda w: pl.BlockSpec((tm, w), lambda i: (i, 0))
    in_specs = [tok(D_MODEL)]
    args = [x2d]
    if ln is not None:
        in_specs += [_resident((1, D_MODEL)), _resident((1, D_MODEL))]
        args += list(ln)
    in_specs += [_resident(wtok.shape), _resident(wt.shape), _resident(wg.shape), _resident(bg.shape),
                 pl.BlockSpec((tm, B_QK // 2), lambda i: (i % nt, 0)),
                 pl.BlockSpec((tm, B_QK // 2), lambda i: (i % nt, 0)),
                 pl.BlockSpec((B_QK // 2, tm), lambda i: (0, i % nt)),
                 pl.BlockSpec((B_QK // 2, tm), lambda i: (0, i % nt))]
    args += [wtok, wt, wg, bg, cos, sin, cost, sint]
    out_shape = [jax.ShapeDtypeStruct((tokens, A_QK_W), BF16),
                 jax.ShapeDtypeStruct((tokens, V_W), BF16),
                 jax.ShapeDtypeStruct((tokens, B_QK_W), BF16),
                 jax.ShapeDtypeStruct((tokens, V_W), BF16),
                 jax.ShapeDtypeStruct((batch, KT_ROWS, seq), BF16),
                 jax.ShapeDtypeStruct((batch, N_GATES, seq), F32)]
    out_specs = [tok(A_QK_W), tok(V_W), tok(B_QK_W), tok(V_W),
                 pl.BlockSpec((None, KT_ROWS, tm), lambda i: (i // nt, 0, i % nt)),
                 pl.BlockSpec((None, N_GATES, tm), lambda i: (i // nt, 0, i % nt))]
    if ln is not None:
        out_shape.append(jax.ShapeDtypeStruct((tokens, D_MODEL), F32))
        out_specs.append(tok(D_MODEL))
    return pl.pallas_call(
        functools.partial(_proj_kernel, apply_ln=ln is not None),
        grid=(tokens // tm,),
        in_specs=in_specs,
        out_specs=out_specs,
        out_shape=out_shape,
        compiler_params=pltpu.CompilerParams(dimension_semantics=("parallel",),
                                             vmem_limit_bytes=VMEM_LIMIT),
        name="proj",
    )(*args)


def _cumsum_lanes(x, reverse):
    lane = lax.broadcasted_iota(jnp.int32, x.shape, 1)
    k = 1
    while k < CHUNK:
        if reverse:
            x = x + jnp.where(lane < CHUNK - k, pltpu.roll(x, CHUNK - k, axis=1), 0.0)
        else:
            x = x + jnp.where(lane >= k, pltpu.roll(x, k, axis=1), 0.0)
        k *= 2
    return x


def _mixer_kernel(lg_ref,
                  qa_f, va_f, qb_f, vb_f, kt_f, g_f,
                  qa_b, va_b, qb_b, vb_b, kt_b, g_b,
                  hf_ref, hb_ref, rf_ref, rb_ref,
                  c_ref, nv_ref, m_ref, r_ref, dm_ref, wi_ref, rc_ref):
    L = CHUNK
    t_i = lax.broadcasted_iota(jnp.int32, (L, L), 0)
    s_i = lax.broadcasted_iota(jnp.int32, (L, L), 1)

    @pl.when(pl.program_id(1) == 0)
    def _init():
        c_ref[...] = jnp.zeros_like(c_ref)
        nv_ref[...] = jnp.zeros_like(nv_ref)
        m_ref[...] = jnp.zeros_like(m_ref)
        r_ref[...] = jnp.zeros_like(r_ref)
        tcol = lax.broadcasted_iota(jnp.int32, (L, L), 0).astype(F32)
        srow = lax.broadcasted_iota(jnp.int32, (8, L), 1).astype(F32)
        row8 = lax.broadcasted_iota(jnp.int32, (8, L), 0)
        for d in range(2):
            diff = ((t_i - s_i) if d == 0 else (s_i - t_i)).astype(F32)
            for h in range(HEADS):
                lg = lg_ref[d, h]
                dm_ref[d * HEADS + h] = jnp.where(diff >= 0.0, jnp.exp(lg * jnp.maximum(diff, 0.0)), 0.0)
                if d == 0:
                    wi = jnp.exp(lg * (tcol + 1.0))
                    ws = jnp.exp(lg * (L - 1.0 - srow))
                else:
                    wi = jnp.exp(lg * (L - tcol))
                    ws = jnp.exp(lg * srow)
                wi_ref[d * HEADS + h] = wi
                chunk_dec = jnp.exp(jnp.full((8, L), lg * float(L), F32))
                rc_ref[d * HEADS + h] = jnp.where(row8 == 0, ws, chunk_dec)

    eye = t_i == s_i
    dirs = ((qa_f, va_f, qb_f, vb_f, kt_f, g_f, hf_ref, rf_ref),
            (qa_b, va_b, qb_b, vb_b, kt_b, g_b, hb_ref, rb_ref))
    for d, (qa, va, qb, vb, kt, g, h_out, r_out) in enumerate(dirs):
        tri = (s_i <= t_i) if d == 0 else (s_i >= t_i)
        gates = g[d * 8:(d + 1) * 8, :]
        cums = _cumsum_lanes(gates, reverse=(d == 1))
        for h in range(HEADS):
            idx = d * HEADS + h
            q = qa[:, h * A_QK:(h + 1) * A_QK]
            k_t = kt[h * A_QK:(h + 1) * A_QK, :]
            v = va[:, h * HEAD_V:(h + 1) * HEAD_V]
            i_row = gates[h:h + 1, :]
            cb_row = cums[HEADS + h:HEADS + h + 1, :]
            cb_col = jnp.sum(jnp.where(eye, cb_row, 0.0), axis=1, keepdims=True)
            m_old = m_ref[idx][0:1, 0:1]
            c_old = c_ref[idx]
            nv_old = nv_ref[idx]

            draw = cb_col + (i_row - cb_row)
            m_d = jnp.max(jnp.where(tri, draw, -jnp.inf), axis=1, keepdims=True)
            inter = cb_col + m_old
            m_t = jnp.maximum(inter, m_d)
            w_inter = jnp.exp(inter - m_t)
            sx = _dot(q, jnp.concatenate([k_t, nv_old.astype(BF16)], axis=1))
            p = sx[:, :L] * jnp.where(tri, jnp.exp(draw - m_t), 0.0)
            q_nv = sx[:, L:L + 1]
            num = _dot(p.astype(BF16), v) + w_inter * _dot(q, c_old.astype(BF16))
            den = jnp.sum(p, axis=1, keepdims=True) + w_inter * q_nv
            inv = 1.0 / jnp.maximum(jnp.abs(den), jnp.exp(-m_t))
            h_out[:, h * HEAD_V:(h + 1) * HEAD_V] = num * inv

            tot = cb_row[:, L - 1:L] if d == 0 else cb_row[:, 0:1]
            g_row = tot - cb_row + i_row
            m_new = jnp.maximum(tot + m_old, jnp.max(g_row, axis=1, keepdims=True))
            w_s = jnp.exp(g_row - m_new)
            dec = jnp.exp(tot + m_old - m_new)
            kw_t = k_t.astype(F32) * w_s
            c_ref[idx] = dec * c_old + _dot(kw_t.astype(BF16), v)
            nv_ref[idx] = dec * nv_old + jnp.sum(kw_t, axis=1, keepdims=True)
            m_ref[idx] = jnp.broadcast_to(m_new, m_ref.shape[1:])

            q = qb[:, h * B_QK:(h + 1) * B_QK]
            k_t = kt[A_QK_W + h * B_QK:A_QK_W + (h + 1) * B_QK, :]
            v = vb[:, h * HEAD_V:(h + 1) * HEAD_V]
            r_old = r_ref[idx]
            a = _dot(q, k_t) * dm_ref[idx]
            wi = wi_ref[idx]
            o = _dot(a.astype(BF16), v) + jnp.concatenate([wi, wi], axis=1) * _dot(q, r_old.astype(BF16))
            r_out[:, h * HEAD_V:(h + 1) * HEAD_V] = o
            rc = rc_ref[idx]
            w_state = rc[0:1, :]
            chunk_dec = rc[1:2, 0:1]
            r_ref[idx] = chunk_dec * r_old + _dot((k_t.astype(F32) * w_state).astype(BF16), v)


def _mixer_call(lg, qa, va, qb, vb, kt, gt, batch, seq):
    L = CHUNK
    n = seq // L
    qa, va, qb, vb = (a.reshape(batch, seq, a.shape[-1]) for a in (qa, va, qb, vb))

    def specs(chunk_of):
        tokm = lambda w: pl.BlockSpec((None, L, w), lambda b, c: (b, chunk_of(c), 0))
        featm = lambda r: pl.BlockSpec((None, r, L), lambda b, c: (b, 0, chunk_of(c)))
        return [tokm(A_QK_W), tokm(V_W), tokm(B_QK_W), tokm(V_W), featm(KT_ROWS), featm(N_GATES)]

    fwd = lambda c: c
    bwd = lambda c: n - 1 - c
    out_f = pl.BlockSpec((None, L, V_W), lambda b, c: (b, c, 0))
    out_b = pl.BlockSpec((None, L, V_W), lambda b, c: (b, n - 1 - c, 0))
    out_sds = jax.ShapeDtypeStruct((batch, seq, V_W), F32)
    nstate = 2 * HEADS
    return pl.pallas_call(
        _mixer_kernel,
        grid=(batch, n),
        in_specs=[pl.BlockSpec(memory_space=pltpu.SMEM)] + specs(fwd) + specs(bwd),
        out_specs=[out_f, out_b, out_f, out_b],
        out_shape=[out_sds] * 4,
        scratch_shapes=[pltpu.VMEM((nstate, A_QK, HEAD_V), F32),
                        pltpu.VMEM((nstate, A_QK, L), F32),
                        pltpu.VMEM((nstate, 8, L), F32),
                        pltpu.VMEM((nstate, B_QK, HEAD_V), F32),
                        pltpu.VMEM((nstate, L, L), F32),
                        pltpu.VMEM((nstate, L, L), F32),
                        pltpu.VMEM((nstate, 8, L), F32)],
        compiler_params=pltpu.CompilerParams(dimension_semantics=("parallel", "arbitrary"),
                                             vmem_limit_bytes=VMEM_LIMIT),
        name="mixer",
    )(lg, qa, va, qb, vb, kt, gt, qa, va, qb, vb, kt, gt)


def _head_norm(h):
    mu = jnp.mean(h, axis=-1, keepdims=True)
    hc = h - mu
    var = jnp.mean(hc * hc, axis=-1, keepdims=True)
    return hc * lax.rsqrt(var + LN_EPS)


def _merge_kernel(x_ref, hf_ref, hb_ref, rf_ref, rb_ref, w4_ref, ga_ref, gb_ref, wo_ref, lng_ref, lnb_ref,
                  o_ref, merged_ref, *, alpha):
    x = x_ref[...]
    xb = x.astype(BF16)
    for h in range(HEADS):
        sl = slice(h * HEAD_V, (h + 1) * HEAD_V)
        col = lambda piece: slice(piece * V_W + h * HEAD_V, piece * V_W + (h + 1) * HEAD_V)
        o_gate = jax.nn.sigmoid(_dot(xb, w4_ref[:, col(0)]))
        y_a = _head_norm((hf_ref[:, sl] + hb_ref[:, sl]) * o_gate) * ga_ref[:, sl]
        swish = jax.nn.silu(_dot(xb, w4_ref[:, col(1)]))
        y_b = _head_norm(rf_ref[:, sl] + rb_ref[:, sl]) * gb_ref[:, sl] * swish
        m_a = jax.nn.sigmoid(_dot(xb, w4_ref[:, col(2)]))
        m_b = jax.nn.sigmoid(_dot(xb, w4_ref[:, col(3)]))
        merged_ref[:, sl] = (m_a * y_a + m_b * y_b).astype(BF16)
    z = alpha * x + _dot(merged_ref[...], wo_ref[...])
    o_ref[...] = _layer_norm(z, lng_ref[...], lnb_ref[...])


def _merge_call(x2d, hf, hb, rf, rb, w4, g_a, g_b, wo, ln_g, ln_b, alpha):
    tokens = x2d.shape[0]
    tm = MERGE_TM
    tok = pl.BlockSpec((tm, D_MODEL), lambda i: (i, 0))
    flat = lambda a: a.reshape(tokens, V_W)
    return pl.pallas_call(
        functools.partial(_merge_kernel, alpha=alpha),
        grid=(tokens // tm,),
        in_specs=[tok, tok, tok, tok, tok, _resident(w4.shape), _resident(g_a.shape), _resident(g_b.shape),
                  _resident(wo.shape), _resident(ln_g.shape), _resident(ln_b.shape)],
        out_specs=tok,
        out_shape=jax.ShapeDtypeStruct((tokens, D_MODEL), F32),
        scratch_shapes=[pltpu.VMEM((tm, V_W), BF16)],
        compiler_params=pltpu.CompilerParams(dimension_semantics=("parallel",),
                                             vmem_limit_bytes=VMEM_LIMIT),
        name="merge",
    )(x2d, flat(hf), flat(hb), flat(rf), flat(rb), w4, g_a, g_b, wo, ln_g, ln_b)


def _ffn_kernel(x_ref, xp_ref, xn_ref, wup_ref, cw_ref, cb_ref, wdn_ref, lng_ref, lnb_ref, o_ref, acc_ref,
                *, alpha, tiles_per_seq):
    tm = FFN_TM
    rows = tm + 2 * HALO
    pos = lax.rem(pl.program_id(0), tiles_per_seq)
    x = x_ref[...]
    x_prev = jnp.where(pos == 0, 0.0, xp_ref[...])
    x_next = jnp.where(pos == tiles_per_seq - 1, 0.0, xn_ref[...])
    xe = jnp.concatenate([x_prev, x, x_next], axis=0).astype(BF16)

    def conv(hu, c0):
        w = cw_ref[:, c0:c0 + FFN_TF]
        before = pltpu.roll(hu, 1, axis=0)[HALO:HALO + tm]
        after = pltpu.roll(hu, rows - 1, axis=0)[HALO:HALO + tm]
        return (w[0:1] * before + w[1:2] * hu[HALO:HALO + tm] + w[2:3] * after
                + cb_ref[:, c0:c0 + FFN_TF])

    for j in range(D_FF // FFN_TF):
        c_val = j * FFN_TF
        c_gate = D_FF + j * FFN_TF
        val = conv(_dot(xe, wup_ref[:, c_val:c_val + FFN_TF]), c_val)
        gate = conv(_dot(xe, wup_ref[:, c_gate:c_gate + FFN_TF]), c_gate)
        f = (jax.nn.gelu(gate) * val).astype(BF16)
        part = _dot(f, wdn_ref[c_val:c_val + FFN_TF, :])
        if j == 0:
            acc_ref[...] = part
        else:
            acc_ref[...] += part
    o_ref[...] = _layer_norm(alpha * x + acc_ref[...], lng_ref[...], lnb_ref[...])


def _ffn_call(x2d, seq, wup, cw, cb, wdn, ln_g, ln_b, alpha):
    tokens = x2d.shape[0]
    tm = FFN_TM
    per = tm // HALO
    last = tokens // HALO - 1
    tok = pl.BlockSpec((tm, D_MODEL), lambda i: (i, 0))
    return pl.pallas_call(
        functools.partial(_ffn_kernel, alpha=alpha, tiles_per_seq=seq // tm),
        grid=(tokens // tm,),
        in_specs=[tok,
                  pl.BlockSpec((HALO, D_MODEL), lambda i: (jnp.maximum(i * per - 1, 0), 0)),
                  pl.BlockSpec((HALO, D_MODEL), lambda i: (jnp.minimum((i + 1) * per, last), 0)),
                  _resident(wup.shape), _resident(cw.shape), _resident(cb.shape), _resident(wdn.shape),
                  _resident(ln_g.shape), _resident(ln_b.shape)],
        out_specs=tok,
        out_shape=jax.ShapeDtypeStruct((tokens, D_MODEL), F32),
        scratch_shapes=[pltpu.VMEM((tm, D_MODEL), F32)],
        compiler_params=pltpu.CompilerParams(dimension_semantics=("parallel",),
                                             vmem_limit_bytes=VMEM_LIMIT),
        name="ffn",
    )(x2d, x2d, x2d, wup, cw, cb, wdn, ln_g, ln_b)


def _rope_tables(seq):
    inv_freq = ROPE_BASE ** (-jnp.arange(0, B_QK, 2, dtype=F32) / B_QK)
    ang = jnp.arange(seq, dtype=F32)[:, None] * inv_freq[None, :]
    cos = jnp.cos(ang)
    sin = jnp.sin(ang)
    return cos, sin, cos.T, sin.T


def _layer_weights(l, w_in, b_if, w_o, w_up, w_down):
    w = w_in[l]
    cols = lambda o, n: w[:, o:o + n]
    wtok = jnp.concatenate([cols(O_QA, A_QK_W), cols(O_VA, V_W), cols(O_QB, B_QK_W), cols(O_VB, V_W)],
                           axis=1).astype(BF16)
    wt = jnp.concatenate([cols(O_KA, A_QK_W), cols(O_KB, B_QK_W)], axis=1).T.astype(BF16)
    wg = cols(O_G, N_GATES).T.astype(BF16)
    bg = b_if[l].astype(F32).reshape(N_GATES, 1)
    w4 = jnp.concatenate([cols(O_OA, V_W), cols(O_GB, V_W), cols(O_MA, D_MODEL), cols(O_MB, D_MODEL)],
                         axis=1).astype(BF16)
    return wtok, wt, wg, bg, w4, w_o[l].astype(BF16), w_up[l].astype(BF16), w_down[l].astype(BF16)


def _trunk(x, ln_in, layers, alpha):
    batch, seq, _ = x.shape
    assert seq % PROJ_TM == 0 and seq % FFN_TM == 0 and seq % CHUNK == 0
    rope = _rope_tables(seq)
    x2d = x.reshape(batch * seq, D_MODEL)
    for l, lw in enumerate(layers):
        (wtok, wt, wg, bg, w4, wo, wup, wdn, lg, g_a, g_b, ln1, cw, cb, ln2) = lw
        outs = _proj_call(x2d, batch, seq, ln_in if l == 0 else None, wtok, wt, wg, bg, rope)
        qa, va, qb, vb, kt, gt = outs[:6]
        if l == 0:
            x2d = outs[6]
        hf, hb, rf, rb = _mixer_call(lg, qa, va, qb, vb, kt, gt, batch, seq)
        x2d = _merge_call(x2d, hf, hb, rf, rb, w4, g_a, g_b, wo, ln1[0], ln1[1], alpha)
        x2d = _ffn_call(x2d, seq, wup, cw, cb, wdn, ln2[0], ln2[1], alpha)
    return x2d.reshape(batch, seq, D_MODEL)


def kernel(x_prompt, x_sample, ln_in_g, ln_in_b, w_in, b_if, ret_log_decay, mlstm_norm_g, ret_norm_g, w_o, ln1_g, ln1_b, w_up, conv_w, conv_b, w_down, ln2_g, ln2_b):
    depth = w_in.shape[0]
    alpha = (2 * depth) ** 0.25
    row = lambda a: a.astype(F32).reshape(1, -1)
    layers = []
    for l in range(depth):
        layers.append(_layer_weights(l, w_in, b_if, w_o, w_up, w_down) + (
            ret_log_decay[l].astype(F32), row(mlstm_norm_g[l]), row(ret_norm_g[l]),
            (row(ln1_g[l]), row(ln1_b[l])), conv_w[l].astype(F32), row(conv_b[l]),
            (row(ln2_g[l]), row(ln2_b[l]))))
    ln_in = (row(ln_in_g), row(ln_in_b))
    return (_trunk(x_prompt, ln_in, layers, alpha), _trunk(x_sample, ln_in, layers, alpha))
```

```python
import functools

import jax
import jax.numpy as jnp
from jax import lax
from jax.experimental import pallas as pl
from jax.experimental.pallas import tpu as pltpu

F32 = jnp.float32
BF16 = jnp.bfloat16

D_MODEL = 1024
HEADS = 4
A_QK = 128
HEAD_V = 256
B_QK = 256
D_FF = 2816
MIX_L = 256
LANES = 128
LOG2E = 1.4426950408889634
LN_EPS = 1e-5
ROPE_BASE = 10000.0

A_QK_W = HEADS * A_QK
V_W = HEADS * HEAD_V
B_QK_W = HEADS * B_QK
N_GATES = 16
N_GROWS = 24
KT_ROWS = A_QK_W + B_QK_W

_SPLITS = (A_QK_W, A_QK_W, V_W, V_W, N_GATES, B_QK_W, B_QK_W, V_W, V_W, D_MODEL, D_MODEL)
_OFF = [0]
for _w in _SPLITS:
    _OFF.append(_OFF[-1] + _w)
(O_QA, O_KA, O_VA, O_OA, O_G, O_QB, O_KB, O_VB, O_GB, O_MA, O_MB, _) = _OFF

PROJ_TM = 512
MERGE_TM = 256
FFN_TM = 512
FFN_TF = 256
HALO = 8
VMEM_LIMIT = 56 * 1024 * 1024

_NT = (((1,), (1,)), ((), ()))


def _dot(a, b):
    return jnp.dot(a, b, preferred_element_type=F32)


def _dot_nt(a, b):
    return lax.dot_general(a, b, _NT, preferred_element_type=F32)


def _layer_norm(x, g, b):
    mu = jnp.mean(x, axis=-1, keepdims=True)
    xc = x - mu
    var = jnp.mean(xc * xc, axis=-1, keepdims=True)
    return xc * lax.rsqrt(var + LN_EPS) * g + b


def _resident(shape):
    nd = len(shape)
    return pl.BlockSpec(shape, lambda *_: (0,) * nd, pipeline_mode=pl.Buffered(1))


def _scan_chunks(x, op, identity, reverse):
    per_chunk = MIX_L // LANES
    blocks = [x[:, j * LANES:(j + 1) * LANES] for j in range(x.shape[1] // LANES)]
    lane = lax.broadcasted_iota(jnp.int32, blocks[0].shape, 1)
    for j, y in enumerate(blocks):
        k = 1
        while k < LANES:
            if reverse:
                y = op(y, jnp.where(lane < LANES - k, pltpu.roll(y, LANES - k, axis=1), identity))
            else:
                y = op(y, jnp.where(lane >= k, pltpu.roll(y, k, axis=1), identity))
            k *= 2
        blocks[j] = y
    for c0 in range(0, len(blocks), per_chunk):
        if reverse:
            for j in range(c0 + per_chunk - 2, c0 - 1, -1):
                blocks[j] = op(blocks[j], blocks[j + 1][:, 0:1])
        else:
            for j in range(c0 + 1, c0 + per_chunk):
                blocks[j] = op(blocks[j], blocks[j - 1][:, LANES - 1:LANES])
    return jnp.concatenate(blocks, axis=1)


def _proj_kernel(*refs, apply_ln):
    if apply_ln:
        (x_ref, lng_ref, lnb_ref, wtok_ref, wt_ref, wg_ref, bg_ref, cos_ref, sin_ref, cost_ref, sint_ref,
         qa_ref, va_ref, qb_ref, vb_ref, kt_ref, gt_ref, xn_ref) = refs
        x = _layer_norm(x_ref[...], lng_ref[...], lnb_ref[...])
        xn_ref[...] = x
    else:
        (x_ref, wtok_ref, wt_ref, wg_ref, bg_ref, cos_ref, sin_ref, cost_ref, sint_ref,
         qa_ref, va_ref, qb_ref, vb_ref, kt_ref, gt_ref) = refs
        x = x_ref[...]
    xb = x.astype(BF16)

    qa_ref[...] = (_dot(xb, wtok_ref[:, 0:A_QK_W]) * (A_QK ** -0.5)).astype(BF16)
    half = V_W // 2
    for j in range(2):
        c0 = A_QK_W + j * half
        va_ref[:, j * half:(j + 1) * half] = _dot(xb, wtok_ref[:, c0:c0 + half]).astype(BF16)
    cos = cos_ref[...]
    sin = sin_ref[...]
    hq = B_QK // 2
    for h in range(HEADS):
        c0 = A_QK_W + V_W + h * B_QK
        t = _dot(xb, wtok_ref[:, c0:c0 + B_QK])
        t1 = t[:, :hq]
        t2 = t[:, hq:]
        qb_ref[:, h * B_QK:h * B_QK + hq] = (t1 * cos - t2 * sin).astype(BF16)
        qb_ref[:, h * B_QK + hq:(h + 1) * B_QK] = (t1 * sin + t2 * cos).astype(BF16)
    for j in range(2):
        c0 = A_QK_W + V_W + B_QK_W + j * half
        vb_ref[:, j * half:(j + 1) * half] = _dot(xb, wtok_ref[:, c0:c0 + half]).astype(BF16)

    for j in range(2):
        r0 = j * 256
        kt_ref[r0:r0 + 256, :] = _dot_nt(wt_ref[r0:r0 + 256, :], xb).astype(BF16)
    cost = cost_ref[...]
    sint = sint_ref[...]
    for h in range(HEADS):
        r0 = A_QK_W + h * B_QK
        t = _dot_nt(wt_ref[r0:r0 + B_QK, :], xb)
        t1 = t[:hq, :]
        t2 = t[hq:, :]
        kt_ref[r0:r0 + hq, :] = ((t1 * cost - t2 * sint) * (B_QK ** -0.5)).astype(BF16)
        kt_ref[r0 + hq:r0 + B_QK, :] = ((t1 * sint + t2 * cost) * (B_QK ** -0.5)).astype(BF16)

    g = _dot_nt(wg_ref[...], xb) + bg_ref[...]
    row = lax.broadcasted_iota(jnp.int32, g.shape, 0)
    log_sig = jnp.minimum(g, 0.0) - jnp.log1p(jnp.exp(-jnp.abs(g)))
    g = jnp.where((row & HEADS) != 0, log_sig, g)
    first4 = lax.broadcasted_iota(jnp.int32, (8, g.shape[1]), 0) < HEADS
    r_pair, cb_pair, cmr_pair = [], [], []
    for d in range(2):
        gd = g[d * 8:(d + 1) * 8, :]
        cums = _scan_chunks(gd, jnp.add, 0.0, reverse=(d == 1))
        r = gd - pltpu.roll(cums, HEADS, axis=0)
        r_pair.append(r)
        cb_pair.append(cums)
        cmr_pair.append(_scan_chunks(r, jnp.maximum, -jnp.inf, reverse=(d == 1)))
    gt_ref[0:8, :] = jnp.where(first4, r_pair[0], pltpu.roll(r_pair[1], HEADS, axis=0))
    gt_ref[8:16, :] = jnp.where(first4, pltpu.roll(cb_pair[0], HEADS, axis=0), cb_pair[1])
    gt_ref[16:24, :] = jnp.where(first4, cmr_pair[0], pltpu.roll(cmr_pair[1], HEADS, axis=0))


def _proj_call(x2d, batch, seq, ln, wtok, wt, wg, bg, rope):
    tokens = x2d.shape[0]
    tm = PROJ_TM
    nt = seq // tm
    cos, sin, cost, sint = rope
    tok = lambda w: pl.BlockSpec((tm, w), lambda i: (i, 0))
    in_specs = [tok(D_MODEL)]
    args = [x2d]
    if ln is not None:
        in_specs += [_resident((1, D_MODEL)), _resident((1, D_MODEL))]
        args += list(ln)
    in_specs += [_resident(wtok.shape), _resident(wt.shape), _resident(wg.shape), _resident(bg.shape),
                 pl.BlockSpec((tm, B_QK // 2), lambda i: (i % nt, 0)),
                 pl.BlockSpec((tm, B_QK // 2), lambda i: (i % nt, 0)),
                 pl.BlockSpec((B_QK // 2, tm), lambda i: (0, i % nt)),
                 pl.BlockSpec((B_QK // 2, tm), lambda i: (0, i % nt))]
    args += [wtok, wt, wg, bg, cos, sin, cost, sint]
    out_shape = [jax.ShapeDtypeStruct((tokens, A_QK_W), BF16),
                 jax.ShapeDtypeStruct((tokens, V_W), BF16),
                 jax.ShapeDtypeStruct((tokens, B_QK_W), BF16),
                 jax.ShapeDtypeStruct((tokens, V_W), BF16),
                 jax.ShapeDtypeStruct((batch, KT_ROWS, seq), BF16),
                 jax.ShapeDtypeStruct((batch, N_GROWS, seq), F32)]
    out_specs = [tok(A_QK_W), tok(V_W), tok(B_QK_W), tok(V_W),
                 pl.BlockSpec((None, KT_ROWS, tm), lambda i: (i // nt, 0, i % nt)),
                 pl.BlockSpec((None, N_GROWS, tm), lambda i: (i // nt, 0, i % nt))]
    if ln is not None:
        out_shape.append(jax.ShapeDtypeStruct((tokens, D_MODEL), F32))
        out_specs.append(tok(D_MODEL))
    return pl.pallas_call(
        functools.partial(_proj_kernel, apply_ln=ln is not None),
        grid=(tokens // tm,),
        in_specs=in_specs,
        out_specs=out_specs,
        out_shape=out_shape,
        compiler_params=pltpu.CompilerParams(dimension_semantics=("parallel",),
                                             vmem_limit_bytes=VMEM_LIMIT),
        name="proj",
    )(*args)


def _mixer_kernel(lg_ref,
                  qa_f, va_f, qb_f, vb_f, kt_f, g_f,
                  qa_b, va_b, qb_b, vb_b, kt_b, g_b,
                  hf_ref, hb_ref, rf_ref, rb_ref,
                  c_ref, nv_ref, m_ref, r_ref, dm_ref, wi_ref, rc_ref):
    L = MIX_L
    t_i = lax.broadcasted_iota(jnp.int32, (L, L), 0)
    s_i = lax.broadcasted_iota(jnp.int32, (L, L), 1)

    @pl.when(pl.program_id(1) == 0)
    def _init():
        c_ref[...] = jnp.zeros_like(c_ref)
        nv_ref[...] = jnp.zeros_like(nv_ref)
        m_ref[...] = jnp.zeros_like(m_ref)
        r_ref[...] = jnp.zeros_like(r_ref)
        tcol = lax.broadcasted_iota(jnp.int32, (L, LANES), 0).astype(F32)
        srow = lax.broadcasted_iota(jnp.int32, (8, L), 1).astype(F32)
        row8 = lax.broadcasted_iota(jnp.int32, (8, L), 0)
        for d in range(2):
            diff = ((t_i - s_i) if d == 0 else (s_i - t_i)).astype(F32)
            for h in range(HEADS):
                lg = lg_ref[d, h]
                dm_ref[d * HEADS + h] = jnp.where(diff >= 0.0, jnp.exp(lg * jnp.maximum(diff, 0.0)), 0.0)
                if d == 0:
                    wi = jnp.exp(lg * (tcol + 1.0))
                    ws = jnp.exp(lg * (L - 1.0 - srow))
                else:
                    wi = jnp.exp(lg * (L - tcol))
                    ws = jnp.exp(lg * srow)
                wi_ref[d * HEADS + h] = wi
                chunk_dec = jnp.exp(jnp.full((8, L), lg * float(L), F32))
                rc_ref[d * HEADS + h] = jnp.where(row8 == 0, ws, chunk_dec)

    eye = (lax.broadcasted_iota(jnp.int32, (A_QK, LANES), 0)
           == lax.broadcasted_iota(jnp.int32, (A_QK, LANES), 1))
    dirs = ((qa_f, va_f, qb_f, vb_f, kt_f, g_f, hf_ref, rf_ref),
            (qa_b, va_b, qb_b, vb_b, kt_b, g_b, hb_ref, rb_ref))
    bodies = [(d, h) for d in range(2) for h in range(HEADS)]

    def operands(d, h):
        qa, va, qb, vb, kt = dirs[d][:5]
        return dict(
            q_a=qa[:, h * A_QK:(h + 1) * A_QK],
            kt_a=kt[h * A_QK:(h + 1) * A_QK, :],
            v_a=va[:, h * HEAD_V:(h + 1) * HEAD_V],
            q_b=qb[:, h * B_QK:(h + 1) * B_QK],
            kt_b=kt[A_QK_W + h * B_QK:A_QK_W + (h + 1) * B_QK, :],
            v_b=vb[:, h * HEAD_V:(h + 1) * HEAD_V])


    prep = {}
    for d, h in bodies:
        idx = d * HEADS + h
        g = dirs[d][5]
        end = 0 if d == 1 else L - 1
        r_row = g[idx:idx + 1, :]
        cb_row = g[8 + idx:8 + idx + 1, :]
        cmr_row = g[16 + idx:16 + idx + 1, :]
        cb_col = jnp.transpose(cb_row)
        cmr_col = jnp.transpose(cmr_row)
        m_old = m_ref[idx][0:1, 0:1]
        big_m = jnp.maximum(m_old, cmr_col)
        m_top = jnp.maximum(m_old, cmr_row[:, end:end + 1])
        prep[(d, h)] = dict(
            r2_row=r_row * LOG2E, big_m2=big_m * LOG2E,
            w_inter=jnp.exp(m_old - big_m), floor=jnp.exp(-(cb_col + big_m)),
            w_s=jnp.exp(r_row - m_top), dec=jnp.exp(m_old - m_top),
            m_new=cb_row[:, end:end + 1] + m_top)

    st1 = {}
    for d, h in bodies:
        idx = d * HEADS + h
        o = operands(d, h)
        st1[(d, h)] = dict(
            sa=_dot(o["q_a"], o["kt_a"]), sb=_dot(o["q_b"], o["kt_b"]),
            qc=_dot(o["q_a"], c_ref[idx].astype(BF16)), qr=_dot(o["q_b"], r_ref[idx].astype(BF16)))

    st2 = {}
    for d, h in bodies:
        idx = d * HEADS + h
        o = operands(d, h)
        pr = prep[(d, h)]
        tri = (s_i >= t_i) if d == 1 else (s_i <= t_i)
        p = st1[(d, h)]["sa"] * jnp.where(tri, jnp.exp2(pr["r2_row"] - pr["big_m2"]), 0.0)
        a = st1[(d, h)]["sb"] * dm_ref[idx]
        kw_t = o["kt_a"].astype(F32) * pr["w_s"]
        kbw_t = o["kt_b"].astype(F32) * rc_ref[idx][0:1, :]
        st2[(d, h)] = dict(pb=p.astype(BF16), psum=jnp.sum(p, axis=1, keepdims=True), ab=a.astype(BF16),
                           kw=kw_t.astype(BF16), kwsum=jnp.sum(kw_t, axis=1, keepdims=True),
                           kbw=kbw_t.astype(BF16))

    st3 = {}
    for d, h in bodies:
        o = operands(d, h)
        s2 = st2[(d, h)]
        st3[(d, h)] = dict(pv=_dot(s2["pb"], o["v_a"]), av=_dot(s2["ab"], o["v_b"]),
                           kv_a=_dot(s2["kw"], o["v_a"]), kv_b=_dot(s2["kbw"], o["v_b"]))

    for d, h in bodies:
        idx = d * HEADS + h
        o = operands(d, h)
        pr, s1, s2, s3 = prep[(d, h)], st1[(d, h)], st2[(d, h)], st3[(d, h)]
        h_out, r_out = dirs[d][6], dirs[d][7]
        nv_old = nv_ref[idx]
        nv_row = jnp.sum(jnp.where(eye, nv_old, 0.0), axis=0, keepdims=True)
        q_nv = jnp.sum(o["q_a"].astype(F32) * nv_row, axis=1, keepdims=True)
        num = s3["pv"] + pr["w_inter"] * s1["qc"]
        den = s2["psum"] + pr["w_inter"] * q_nv
        inv = 1.0 / jnp.maximum(jnp.abs(den), pr["floor"])
        h_out[:, h * HEAD_V:(h + 1) * HEAD_V] = num * inv
        c_ref[idx] = pr["dec"] * c_ref[idx] + s3["kv_a"]
        nv_ref[idx] = pr["dec"] * nv_old + s2["kwsum"]
        m_ref[idx] = jnp.broadcast_to(pr["m_new"], m_ref.shape[1:])
        wi = wi_ref[idx]
        r_out[:, h * HEAD_V:(h + 1) * HEAD_V] = s3["av"] + jnp.concatenate([wi, wi], axis=1) * s1["qr"]
        r_ref[idx] = rc_ref[idx][1:2, 0:1] * r_ref[idx] + s3["kv_b"]


def _mixer_call(lg, qa, va, qb, vb, kt, gt, batch, seq):
    L = MIX_L
    n = seq // L
    qa, va, qb, vb = (a.reshape(batch, seq, a.shape[-1]) for a in (qa, va, qb, vb))

    def specs(chunk_of):
        tokm = lambda w: pl.BlockSpec((None, L, w), lambda b, c: (b, chunk_of(c), 0))
        featm = lambda r: pl.BlockSpec((None, r, L), lambda b, c: (b, 0, chunk_of(c)))
        return [tokm(A_QK_W), tokm(V_W), tokm(B_QK_W), tokm(V_W), featm(KT_ROWS), featm(N_GROWS)]

    fwd = lambda c: c
    bwd = lambda c: n - 1 - c
    out_f = pl.BlockSpec((None, L, V_W), lambda b, c: (b, c, 0))
    out_b = pl.BlockSpec((None, L, V_W), lambda b, c: (b, n - 1 - c, 0))
    out_sds = jax.ShapeDtypeStruct((batch, seq, V_W), F32)
    nstate = 2 * HEADS
    return pl.pallas_call(
        _mixer_kernel,
        grid=(batch, n),
        in_specs=[pl.BlockSpec(memory_space=pltpu.SMEM)] + specs(fwd) + specs(bwd),
        out_specs=[out_f, out_b, out_f, out_b],
        out_shape=[out_sds] * 4,
        scratch_shapes=[pltpu.VMEM((nstate, A_QK, HEAD_V), F32),
                        pltpu.VMEM((nstate, A_QK, LANES), F32),
                        pltpu.VMEM((nstate, 8, LANES), F32),
                        pltpu.VMEM((nstate, B_QK, HEAD_V), F32),
                        pltpu.VMEM((nstate, L, L), F32),
                        pltpu.VMEM((nstate, L, LANES), F32),
                        pltpu.VMEM((nstate, 8, L), F32)],
        compiler_params=pltpu.CompilerParams(dimension_semantics=("parallel", "arbitrary"),
                                             vmem_limit_bytes=VMEM_LIMIT),
        name="mixer",
    )(lg, qa, va, qb, vb, kt, gt, qa, va, qb, vb, kt, gt)


def _head_norm(h):
    mu = jnp.mean(h, axis=-1, keepdims=True)
    hc = h - mu
    var = jnp.mean(hc * hc, axis=-1, keepdims=True)
    return hc * lax.rsqrt(var + LN_EPS)


def _merge_kernel(x_ref, hf_ref, hb_ref, rf_ref, rb_ref, w4_ref, ga_ref, gb_ref, wo_ref, lng_ref, lnb_ref,
                  o_ref, merged_ref, *, alpha):
    x = x_ref[...]
    xb = x.astype(BF16)
    for h in range(HEADS):
        sl = slice(h * HEAD_V, (h + 1) * HEAD_V)
        col = lambda piece: slice(piece * V_W + h * HEAD_V, piece * V_W + (h + 1) * HEAD_V)
        o_gate = jax.nn.sigmoid(_dot(xb, w4_ref[:, col(0)]))
        y_a = _head_norm((hf_ref[:, sl] + hb_ref[:, sl]) * o_gate) * ga_ref[:, sl]
        swish = jax.nn.silu(_dot(xb, w4_ref[:, col(1)]))
        y_b = _head_norm(rf_ref[:, sl] + rb_ref[:, sl]) * gb_ref[:, sl] * swish
        m_a = jax.nn.sigmoid(_dot(xb, w4_ref[:, col(2)]))
        m_b = jax.nn.sigmoid(_dot(xb, w4_ref[:, col(3)]))
        merged_ref[:, sl] = (m_a * y_a + m_b * y_b).astype(BF16)
    z = alpha * x + _dot(merged_ref[...], wo_ref[...])
    o_ref[...] = _layer_norm(z, lng_ref[...], lnb_ref[...])


def _merge_call(x2d, hf, hb, rf, rb, w4, g_a, g_b, wo, ln_g, ln_b, alpha):
    tokens = x2d.shape[0]
    tm = MERGE_TM
    tok = pl.BlockSpec((tm, D_MODEL), lambda i: (i, 0))
    flat = lambda a: a.reshape(tokens, V_W)
    return pl.pallas_call(
        functools.partial(_merge_kernel, alpha=alpha),
        grid=(tokens // tm,),
        in_specs=[tok, tok, tok, tok, tok, _resident(w4.shape), _resident(g_a.shape), _resident(g_b.shape),
                  _resident(wo.shape), _resident(ln_g.shape), _resident(ln_b.shape)],
        out_specs=tok,
        out_shape=jax.ShapeDtypeStruct((tokens, D_MODEL), F32),
        scratch_shapes=[pltpu.VMEM((tm, V_W), BF16)],
        compiler_params=pltpu.CompilerParams(dimension_semantics=("parallel",),
                                             vmem_limit_bytes=VMEM_LIMIT),
        name="merge",
    )(x2d, flat(hf), flat(hb), flat(rf), flat(rb), w4, g_a, g_b, wo, ln_g, ln_b)


def _ffn_kernel(x_ref, xp_ref, xn_ref, wup_ref, cw_ref, cb_ref, wdn_ref, lng_ref, lnb_ref, o_ref, acc_ref,
                *, alpha, tiles_per_seq):
    tm = FFN_TM
    rows = tm + 2 * HALO
    pos = lax.rem(pl.program_id(0), tiles_per_seq)
    x = x_ref[...]
    x_prev = jnp.where(pos == 0, 0.0, xp_ref[...])
    x_next = jnp.where(pos == tiles_per_seq - 1, 0.0, xn_ref[...])
    xe = jnp.concatenate([x_prev, x, x_next], axis=0).astype(BF16)

    def conv(hu, c0):
        w = cw_ref[:, c0:c0 + FFN_TF]
        before = pltpu.roll(hu, 1, axis=0)[HALO:HALO + tm]
        after = pltpu.roll(hu, rows - 1, axis=0)[HALO:HALO + tm]
        return (w[0:1] * before + w[1:2] * hu[HALO:HALO + tm] + w[2:3] * after
                + cb_ref[:, c0:c0 + FFN_TF])

    def up(j):
        c_val = j * FFN_TF
        c_gate = D_FF + j * FFN_TF
        return (_dot(xe, wup_ref[:, c_val:c_val + FFN_TF]), _dot(xe, wup_ref[:, c_gate:c_gate + FFN_TF]))

    n_chunks = D_FF // FFN_TF
    nxt = up(0)
    for j in range(n_chunks):
        hu_val, hu_gate = nxt
        if j + 1 < n_chunks:
            nxt = up(j + 1)
        c_val = j * FFN_TF
        f = (jax.nn.gelu(conv(hu_gate, D_FF + c_val)) * conv(hu_val, c_val)).astype(BF16)
        part = _dot(f, wdn_ref[c_val:c_val + FFN_TF, :])
        if j == 0:
            acc_ref[...] = part
        else:
            acc_ref[...] += part
    o_ref[...] = _layer_norm(alpha * x + acc_ref[...], lng_ref[...], lnb_ref[...])


def _ffn_call(x2d, seq, wup, cw, cb, wdn, ln_g, ln_b, alpha):
    tokens = x2d.shape[0]
    tm = FFN_TM
    per = tm // HALO
    last = tokens // HALO - 1
    tok = pl.BlockSpec((tm, D_MODEL), lambda i: (i, 0))
    return pl.pallas_call(
        functools.partial(_ffn_kernel, alpha=alpha, tiles_per_seq=seq // tm),
        grid=(tokens // tm,),
        in_specs=[tok,
                  pl.BlockSpec((HALO, D_MODEL), lambda i: (jnp.maximum(i * per - 1, 0), 0)),
                  pl.BlockSpec((HALO, D_MODEL), lambda i: (jnp.minimum((i + 1) * per, last), 0)),
                  _resident(wup.shape), _resident(cw.shape), _resident(cb.shape), _resident(wdn.shape),
                  _resident(ln_g.shape), _resident(ln_b.shape)],
        out_specs=tok,
        out_shape=jax.ShapeDtypeStruct((tokens, D_MODEL), F32),
        scratch_shapes=[pltpu.VMEM((tm, D_MODEL), F32)],
        compiler_params=pltpu.CompilerParams(dimension_semantics=("parallel",),
                                             vmem_limit_bytes=VMEM_LIMIT),
        name="ffn",
    )(x2d, x2d, x2d, wup, cw, cb, wdn, ln_g, ln_b)


def _rope_tables(seq):
    inv_freq = ROPE_BASE ** (-jnp.arange(0, B_QK, 2, dtype=F32) / B_QK)
    ang = jnp.arange(seq, dtype=F32)[:, None] * inv_freq[None, :]
    cos = jnp.cos(ang)
    sin = jnp.sin(ang)
    return cos, sin, cos.T, sin.T


def _layer_weights(l, w_in, b_if, w_o, w_up, w_down):
    w = w_in[l]
    cols = lambda o, n: w[:, o:o + n]
    wtok = jnp.concatenate([cols(O_QA, A_QK_W), cols(O_VA, V_W), cols(O_QB, B_QK_W), cols(O_VB, V_W)],
                           axis=1).astype(BF16)
    wt = jnp.concatenate([cols(O_KA, A_QK_W), cols(O_KB, B_QK_W)], axis=1).T.astype(BF16)
    wg = cols(O_G, N_GATES).T.astype(BF16)
    bg = b_if[l].astype(F32).reshape(N_GATES, 1)
    w4 = jnp.concatenate([cols(O_OA, V_W), cols(O_GB, V_W), cols(O_MA, D_MODEL), cols(O_MB, D_MODEL)],
                         axis=1).astype(BF16)
    return wtok, wt, wg, bg, w4, w_o[l].astype(BF16), w_up[l].astype(BF16), w_down[l].astype(BF16)


def _trunk(x, ln_in, layers, alpha):
    batch, seq, _ = x.shape
    assert seq % PROJ_TM == 0 and seq % FFN_TM == 0 and seq % MIX_L == 0
    rope = _rope_tables(seq)
    x2d = x.reshape(batch * seq, D_MODEL)
    for l, lw in enumerate(layers):
        (wtok, wt, wg, bg, w4, wo, wup, wdn, lg, g_a, g_b, ln1, cw, cb, ln2) = lw
        outs = _proj_call(x2d, batch, seq, ln_in if l == 0 else None, wtok, wt, wg, bg, rope)
        qa, va, qb, vb, kt, gt = outs[:6]
        if l == 0:
            x2d = outs[6]
        hf, hb, rf, rb = _mixer_call(lg, qa, va, qb, vb, kt, gt, batch, seq)
        x2d = _merge_call(x2d, hf, hb, rf, rb, w4, g_a, g_b, wo, ln1[0], ln1[1], alpha)
        x2d = _ffn_call(x2d, seq, wup, cw, cb, wdn, ln2[0], ln2[1], alpha)
    return x2d.reshape(batch, seq, D_MODEL)


def kernel(x_prompt, x_sample, ln_in_g, ln_in_b, w_in, b_if, ret_log_decay, mlstm_norm_g, ret_norm_g, w_o, ln1_g, ln1_b, w_up, conv_w, conv_b, w_down, ln2_g, ln2_b):
    depth = w_in.shape[0]
    alpha = (2 * depth) ** 0.25
    row = lambda a: a.astype(F32).reshape(1, -1)
    layers = []
    for l in range(depth):
        layers.append(_layer_weights(l, w_in, b_if, w_o, w_up, w_down) + (
            ret_log_decay[l].astype(F32), row(mlstm_norm_g[l]), row(ret_norm_g[l]),
            (row(ln1_g[l]), row(ln1_b[l])), conv_w[l].astype(F32), row(conv_b[l]),
            (row(ln2_g[l]), row(ln2_b[l]))))
    ln_in = (row(ln_in_g), row(ln_in_b))
    return (_trunk(x_prompt, ln_in, layers, alpha), _trunk(x_sample, ln_in, layers, alpha))
```

```python
import functools

import jax
import jax.numpy as jnp
from jax import lax
from jax.experimental import pallas as pl
from jax.experimental.pallas import tpu as pltpu

F32 = jnp.float32
BF16 = jnp.bfloat16

D_MODEL = 1024
HEADS = 4
A_QK = 128
HEAD_V = 256
B_QK = 256
D_FF = 2816
MIX_L = 256
LANES = 128
LOG2E = 1.4426950408889634
LN_EPS = 1e-5
ROPE_BASE = 10000.0

A_QK_W = HEADS * A_QK
V_W = HEADS * HEAD_V
B_QK_W = HEADS * B_QK
N_GATES = 16
N_GROWS = 24
KT_ROWS = A_QK_W + B_QK_W

_SPLITS = (A_QK_W, A_QK_W, V_W, V_W, N_GATES, B_QK_W, B_QK_W, V_W, V_W, D_MODEL, D_MODEL)
_OFF = [0]
for _w in _SPLITS:
    _OFF.append(_OFF[-1] + _w)
(O_QA, O_KA, O_VA, O_OA, O_G, O_QB, O_KB, O_VB, O_GB, O_MA, O_MB, _) = _OFF

PROJ_TM = 512
MERGE_TM = 256
FFN_TM = 512
FFN_TF = 256
HALO = 8
VMEM_LIMIT = 56 * 1024 * 1024

_NT = (((1,), (1,)), ((), ()))


def _dot(a, b):
    return jnp.dot(a, b, preferred_element_type=F32)


def _dot_nt(a, b):
    return lax.dot_general(a, b, _NT, preferred_element_type=F32)


def _layer_norm(x, g, b):
    mu = jnp.mean(x, axis=-1, keepdims=True)
    xc = x - mu
    var = jnp.mean(xc * xc, axis=-1, keepdims=True)
    return xc * lax.rsqrt(var + LN_EPS) * g + b


def _resident(shape):
    nd = len(shape)
    return pl.BlockSpec(shape, lambda *_: (0,) * nd, pipeline_mode=pl.Buffered(1))


def _scan_chunks(x, op, identity, reverse):
    per_chunk = MIX_L // LANES
    blocks = [x[:, j * LANES:(j + 1) * LANES] for j in range(x.shape[1] // LANES)]
    lane = lax.broadcasted_iota(jnp.int32, blocks[0].shape, 1)
    for j, y in enumerate(blocks):
        k = 1
        while k < LANES:
            if reverse:
                y = op(y, jnp.where(lane < LANES - k, pltpu.roll(y, LANES - k, axis=1), identity))
            else:
                y = op(y, jnp.where(lane >= k, pltpu.roll(y, k, axis=1), identity))
            k *= 2
        blocks[j] = y
    for c0 in range(0, len(blocks), per_chunk):
        if reverse:
            for j in range(c0 + per_chunk - 2, c0 - 1, -1):
                blocks[j] = op(blocks[j], blocks[j + 1][:, 0:1])
        else:
            for j in range(c0 + 1, c0 + per_chunk):
                blocks[j] = op(blocks[j], blocks[j - 1][:, LANES - 1:LANES])
    return jnp.concatenate(blocks, axis=1)


def _proj_kernel(*refs, apply_ln):
    if apply_ln:
        (x_ref, lng_ref, lnb_ref, wtok_ref, wt_ref, wg_ref, bg_ref, cos_ref, sin_ref, cost_ref, sint_ref,
         qa_ref, va_ref, qb_ref, vb_ref, kt_ref, gt_ref, xn_ref) = refs
        x = _layer_norm(x_ref[...], lng_ref[...], lnb_ref[...])
        xn_ref[...] = x
    else:
        (x_ref, wtok_ref, wt_ref, wg_ref, bg_ref, cos_ref, sin_ref, cost_ref, sint_ref,
         qa_ref, va_ref, qb_ref, vb_ref, kt_ref, gt_ref) = refs
        x = x_ref[...]
    xb = x.astype(BF16)

    qa_ref[...] = (_dot(xb, wtok_ref[:, 0:A_QK_W]) * (A_QK ** -0.5)).astype(BF16)
    half = V_W // 2
    for j in range(2):
        c0 = A_QK_W + j * half
        va_ref[:, j * half:(j + 1) * half] = _dot(xb, wtok_ref[:, c0:c0 + half]).astype(BF16)
    cos = cos_ref[...]
    sin = sin_ref[...]
    hq = B_QK // 2
    for h in range(HEADS):
        c0 = A_QK_W + V_W + h * B_QK
        t = _dot(xb, wtok_ref[:, c0:c0 + B_QK])
        t1 = t[:, :hq]
        t2 = t[:, hq:]
        qb_ref[:, h * B_QK:h * B_QK + hq] = (t1 * cos - t2 * sin).astype(BF16)
        qb_ref[:, h * B_QK + hq:(h + 1) * B_QK] = (t1 * sin + t2 * cos).astype(BF16)
    for j in range(2):
        c0 = A_QK_W + V_W + B_QK_W + j * half
        vb_ref[:, j * half:(j + 1) * half] = _dot(xb, wtok_ref[:, c0:c0 + half]).astype(BF16)

    for j in range(2):
        r0 = j * 256
        kt_ref[r0:r0 + 256, :] = _dot_nt(wt_ref[r0:r0 + 256, :], xb).astype(BF16)
    cost = cost_ref[...]
    sint = sint_ref[...]
    for h in range(HEADS):
        r0 = A_QK_W + h * B_QK
        t = _dot_nt(wt_ref[r0:r0 + B_QK, :], xb)
        t1 = t[:hq, :]
        t2 = t[hq:, :]
        kt_ref[r0:r0 + hq, :] = ((t1 * cost - t2 * sint) * (B_QK ** -0.5)).astype(BF16)
        kt_ref[r0 + hq:r0 + B_QK, :] = ((t1 * sint + t2 * cost) * (B_QK ** -0.5)).astype(BF16)

    g = _dot_nt(wg_ref[...], xb) + bg_ref[...]
    row = lax.broadcasted_iota(jnp.int32, g.shape, 0)
    log_sig = jnp.minimum(g, 0.0) - jnp.log1p(jnp.exp(-jnp.abs(g)))
    gt_ref[...] = jnp.where((row & HEADS) != 0, log_sig, g)


def _gates_kernel(g_ref, o_ref):
    g = g_ref[...]
    first4 = lax.broadcasted_iota(jnp.int32, (8, g.shape[1]), 0) < HEADS
    r_pair, cb_pair, cmr_pair = [], [], []
    for d in range(2):
        gd = g[d * 8:(d + 1) * 8, :]
        cums = _scan_chunks(gd, jnp.add, 0.0, reverse=(d == 1))
        r = gd - pltpu.roll(cums, HEADS, axis=0)
        r_pair.append(r)
        cb_pair.append(cums)
        cmr_pair.append(_scan_chunks(r, jnp.maximum, -jnp.inf, reverse=(d == 1)))
    o_ref[0:8, :] = jnp.where(first4, r_pair[0], pltpu.roll(r_pair[1], HEADS, axis=0))
    o_ref[8:16, :] = jnp.where(first4, pltpu.roll(cb_pair[0], HEADS, axis=0), cb_pair[1])
    o_ref[16:24, :] = jnp.where(first4, cmr_pair[0], pltpu.roll(cmr_pair[1], HEADS, axis=0))


def _gates_call(gt):
    batch, _, seq = gt.shape
    return pl.pallas_call(
        _gates_kernel,
        grid=(batch,),
        in_specs=[pl.BlockSpec((None, N_GATES, seq), lambda b: (b, 0, 0))],
        out_specs=pl.BlockSpec((None, N_GROWS, seq), lambda b: (b, 0, 0)),
        out_shape=jax.ShapeDtypeStruct((batch, N_GROWS, seq), F32),
        compiler_params=pltpu.CompilerParams(dimension_semantics=("parallel",),
                                             vmem_limit_bytes=VMEM_LIMIT),
        name="gates",
    )(gt)


def _proj_call(x2d, batch, seq, ln, wtok, wt, wg, bg, rope):
    tokens = x2d.shape[0]
    tm = PROJ_TM
    nt = seq // tm
    cos, sin, cost, sint = rope
    tok = lambda w: pl.BlockSpec((tm, w), lambda i: (i, 0))
    in_specs = [tok(D_MODEL)]
    args = [x2d]
    if ln is not None:
        in_specs += [_resident((1, D_MODEL)), _resident((1, D_MODEL))]
        args += list(ln)
    in_specs += [_resident(wtok.shape), _resident(wt.shape), _resident(wg.shape), _resident(bg.shape),
                 pl.BlockSpec((tm, B_QK // 2), lambda i: (i % nt, 0)),
                 pl.BlockSpec((tm, B_QK // 2), lambda i: (i % nt, 0)),
                 pl.BlockSpec((B_QK // 2, tm), lambda i: (0, i % nt)),
                 pl.BlockSpec((B_QK // 2, tm), lambda i: (0, i % nt))]
    args += [wtok, wt, wg, bg, cos, sin, cost, sint]
    out_shape = [jax.ShapeDtypeStruct((tokens, A_QK_W), BF16),
                 jax.ShapeDtypeStruct((tokens, V_W), BF16),
                 jax.ShapeDtypeStruct((tokens, B_QK_W), BF16),
                 jax.ShapeDtypeStruct((tokens, V_W), BF16),
                 jax.ShapeDtypeStruct((batch, KT_ROWS, seq), BF16),
                 jax.ShapeDtypeStruct((batch, N_GATES, seq), F32)]
    out_specs = [tok(A_QK_W), tok(V_W), tok(B_QK_W), tok(V_W),
                 pl.BlockSpec((None, KT_ROWS, tm), lambda i: (i // nt, 0, i % nt)),
                 pl.BlockSpec((None, N_GATES, tm), lambda i: (i // nt, 0, i % nt))]
    if ln is not None:
        out_shape.append(jax.ShapeDtypeStruct((tokens, D_MODEL), F32))
        out_specs.append(tok(D_MODEL))
    return pl.pallas_call(
        functools.partial(_proj_kernel, apply_ln=ln is not None),
        grid=(tokens // tm,),
        in_specs=in_specs,
        out_specs=out_specs,
        out_shape=out_shape,
        compiler_params=pltpu.CompilerParams(dimension_semantics=("parallel",),
                                             vmem_limit_bytes=VMEM_LIMIT),
        name="proj",
    )(*args)


def _mixer_kernel(lg_ref,
                  qa_f, va_f, qb_f, vb_f, kt_f, g_f,
                  qa_b, va_b, qb_b, vb_b, kt_b, g_b,
                  hf_ref, hb_ref, rf_ref, rb_ref,
                  c_ref, nv_ref, m_ref, r_ref, dm_ref, wi_ref, rc_ref):
    L = MIX_L
    t_i = lax.broadcasted_iota(jnp.int32, (L, L), 0)
    s_i = lax.broadcasted_iota(jnp.int32, (L, L), 1)

    @pl.when(pl.program_id(1) == 0)
    def _init():
        c_ref[...] = jnp.zeros_like(c_ref)
        nv_ref[...] = jnp.zeros_like(nv_ref)
        m_ref[...] = jnp.zeros_like(m_ref)
        r_ref[...] = jnp.zeros_like(r_ref)
        tcol = lax.broadcasted_iota(jnp.int32, (L, LANES), 0).astype(F32)
        srow = lax.broadcasted_iota(jnp.int32, (8, L), 1).astype(F32)
        row8 = lax.broadcasted_iota(jnp.int32, (8, L), 0)
        for d in range(2):
            diff = ((t_i - s_i) if d == 0 else (s_i - t_i)).astype(F32)
            for h in range(HEADS):
                lg = lg_ref[d, h]
                dm_ref[d * HEADS + h] = jnp.where(diff >= 0.0, jnp.exp(lg * jnp.maximum(diff, 0.0)), 0.0)
                if d == 0:
                    wi = jnp.exp(lg * (tcol + 1.0))
                    ws = jnp.exp(lg * (L - 1.0 - srow))
                else:
                    wi = jnp.exp(lg * (L - tcol))
                    ws = jnp.exp(lg * srow)
                wi_ref[d * HEADS + h] = wi
                chunk_dec = jnp.exp(jnp.full((8, L), lg * float(L), F32))
                rc_ref[d * HEADS + h] = jnp.where(row8 == 0, ws, chunk_dec)

    eye = (lax.broadcasted_iota(jnp.int32, (A_QK, LANES), 0)
           == lax.broadcasted_iota(jnp.int32, (A_QK, LANES), 1))
    dirs = ((qa_f, va_f, qb_f, vb_f, kt_f, g_f, hf_ref, rf_ref),
            (qa_b, va_b, qb_b, vb_b, kt_b, g_b, hb_ref, rb_ref))
    bodies = [(d, h) for d in range(2) for h in range(HEADS)]

    def operands(d, h):
        qa, va, qb, vb, kt = dirs[d][:5]
        return dict(
            q_a=qa[:, h * A_QK:(h + 1) * A_QK],
            kt_a=kt[h * A_QK:(h + 1) * A_QK, :],
            v_a=va[:, h * HEAD_V:(h + 1) * HEAD_V],
            q_b=qb[:, h * B_QK:(h + 1) * B_QK],
            kt_b=kt[A_QK_W + h * B_QK:A_QK_W + (h + 1) * B_QK, :],
            v_b=vb[:, h * HEAD_V:(h + 1) * HEAD_V])


    prep = {}
    for d, h in bodies:
        idx = d * HEADS + h
        g = dirs[d][5]
        end = 0 if d == 1 else L - 1
        r_row = g[idx:idx + 1, :]
        cb_row = g[8 + idx:8 + idx + 1, :]
        cmr_row = g[16 + idx:16 + idx + 1, :]
        cb_col = jnp.transpose(cb_row)
        cmr_col = jnp.transpose(cmr_row)
        m_old = m_ref[idx][0:1, 0:1]
        big_m = jnp.maximum(m_old, cmr_col)
        m_top = jnp.maximum(m_old, cmr_row[:, end:end + 1])
        prep[(d, h)] = dict(
            r2_row=r_row * LOG2E, big_m2=big_m * LOG2E,
            w_inter=jnp.exp(m_old - big_m), floor=jnp.exp(-(cb_col + big_m)),
            w_s=jnp.exp(r_row - m_top), dec=jnp.exp(m_old - m_top),
            m_new=cb_row[:, end:end + 1] + m_top)

    st1 = {}
    for d, h in bodies:
        o = operands(d, h)
        st1[(d, h)] = dict(sa=_dot(o["q_a"], o["kt_a"]), sb=_dot(o["q_b"], o["kt_b"]))

    st2 = {}
    for d, h in bodies:
        idx = d * HEADS + h
        o = operands(d, h)
        pr = prep[(d, h)]
        tri = (s_i >= t_i) if d == 1 else (s_i <= t_i)
        p = st1[(d, h)]["sa"] * jnp.where(tri, jnp.exp2(pr["r2_row"] - pr["big_m2"]), 0.0)
        a = st1[(d, h)]["sb"] * dm_ref[idx]
        kw_t = o["kt_a"].astype(F32) * pr["w_s"]
        kbw_t = o["kt_b"].astype(F32) * rc_ref[idx][0:1, :]
        st2[(d, h)] = dict(pb=p.astype(BF16), psum=jnp.sum(p, axis=1, keepdims=True), ab=a.astype(BF16),
                           kw=kw_t.astype(BF16), kwsum=jnp.sum(kw_t, axis=1, keepdims=True),
                           kbw=kbw_t.astype(BF16))

    for d, h in bodies:
        idx = d * HEADS + h
        o = operands(d, h)
        pr, s2 = prep[(d, h)], st2[(d, h)]
        h_out, r_out = dirs[d][6], dirs[d][7]
        c_old = c_ref[idx]
        r_old = r_ref[idx]
        nv_old = nv_ref[idx]
        nv_row = jnp.sum(jnp.where(eye, nv_old, 0.0), axis=0, keepdims=True)
        q_nv = jnp.sum(o["q_a"].astype(F32) * nv_row, axis=1, keepdims=True)
        den = s2["psum"] + pr["w_inter"] * q_nv
        inv = 1.0 / jnp.maximum(jnp.abs(den), pr["floor"])
        num = _dot(s2["pb"], o["v_a"]) + pr["w_inter"] * _dot(o["q_a"], c_old.astype(BF16))
        h_out[:, h * HEAD_V:(h + 1) * HEAD_V] = num * inv
        wi = wi_ref[idx]
        r_out[:, h * HEAD_V:(h + 1) * HEAD_V] = (
            _dot(s2["ab"], o["v_b"]) + jnp.concatenate([wi, wi], axis=1) * _dot(o["q_b"], r_old.astype(BF16)))
        c_ref[idx] = pr["dec"] * c_old + _dot(s2["kw"], o["v_a"])
        nv_ref[idx] = pr["dec"] * nv_old + s2["kwsum"]
        m_ref[idx] = jnp.broadcast_to(pr["m_new"], m_ref.shape[1:])
        r_ref[idx] = rc_ref[idx][1:2, 0:1] * r_old + _dot(s2["kbw"], o["v_b"])


def _mixer_call(lg, qa, va, qb, vb, kt, gt, batch, seq):
    L = MIX_L
    n = seq // L
    qa, va, qb, vb = (a.reshape(batch, seq, a.shape[-1]) for a in (qa, va, qb, vb))

    def specs(chunk_of):
        tokm = lambda w: pl.BlockSpec((None, L, w), lambda b, c: (b, chunk_of(c), 0))
        featm = lambda r: pl.BlockSpec((None, r, L), lambda b, c: (b, 0, chunk_of(c)))
        return [tokm(A_QK_W), tokm(V_W), tokm(B_QK_W), tokm(V_W), featm(KT_ROWS), featm(N_GROWS)]

    fwd = lambda c: c
    bwd = lambda c: n - 1 - c
    out_f = pl.BlockSpec((None, L, V_W), lambda b, c: (b, c, 0))
    out_b = pl.BlockSpec((None, L, V_W), lambda b, c: (b, n - 1 - c, 0))
    out_sds = jax.ShapeDtypeStruct((batch, seq, V_W), F32)
    nstate = 2 * HEADS
    return pl.pallas_call(
        _mixer_kernel,
        grid=(batch, n),
        in_specs=[pl.BlockSpec(memory_space=pltpu.SMEM)] + specs(fwd) + specs(bwd),
        out_specs=[out_f, out_b, out_f, out_b],
        out_shape=[out_sds] * 4,
        scratch_shapes=[pltpu.VMEM((nstate, A_QK, HEAD_V), F32),
                        pltpu.VMEM((nstate, A_QK, LANES), F32),
                        pltpu.VMEM((nstate, 8, LANES), F32),
                        pltpu.VMEM((nstate, B_QK, HEAD_V), F32),
                        pltpu.VMEM((nstate, L, L), F32),
                        pltpu.VMEM((nstate, L, LANES), F32),
                        pltpu.VMEM((nstate, 8, L), F32)],
        compiler_params=pltpu.CompilerParams(dimension_semantics=("parallel", "arbitrary"),
                                             vmem_limit_bytes=VMEM_LIMIT),
        name="mixer",
    )(lg, qa, va, qb, vb, kt, gt, qa, va, qb, vb, kt, gt)


def _head_norm(h):
    mu = jnp.mean(h, axis=-1, keepdims=True)
    hc = h - mu
    var = jnp.mean(hc * hc, axis=-1, keepdims=True)
    return hc * lax.rsqrt(var + LN_EPS)


def _merge_kernel(x_ref, hf_ref, hb_ref, rf_ref, rb_ref, w4_ref, ga_ref, gb_ref, wo_ref, lng_ref, lnb_ref,
                  o_ref, merged_ref, *, alpha):
    x = x_ref[...]
    xb = x.astype(BF16)
    for h in range(HEADS):
        sl = slice(h * HEAD_V, (h + 1) * HEAD_V)
        col = lambda piece: slice(piece * V_W + h * HEAD_V, piece * V_W + (h + 1) * HEAD_V)
        o_gate = jax.nn.sigmoid(_dot(xb, w4_ref[:, col(0)]))
        y_a = _head_norm((hf_ref[:, sl] + hb_ref[:, sl]) * o_gate) * ga_ref[:, sl]
        swish = jax.nn.silu(_dot(xb, w4_ref[:, col(1)]))
        y_b = _head_norm(rf_ref[:, sl] + rb_ref[:, sl]) * gb_ref[:, sl] * swish
        m_a = jax.nn.sigmoid(_dot(xb, w4_ref[:, col(2)]))
        m_b = jax.nn.sigmoid(_dot(xb, w4_ref[:, col(3)]))
        merged_ref[:, sl] = (m_a * y_a + m_b * y_b).astype(BF16)
    z = alpha * x + _dot(merged_ref[...], wo_ref[...])
    o_ref[...] = _layer_norm(z, lng_ref[...], lnb_ref[...])


def _merge_call(x2d, hf, hb, rf, rb, w4, g_a, g_b, wo, ln_g, ln_b, alpha):
    tokens = x2d.shape[0]
    tm = MERGE_TM
    tok = pl.BlockSpec((tm, D_MODEL), lambda i: (i, 0))
    flat = lambda a: a.reshape(tokens, V_W)
    return pl.pallas_call(
        functools.partial(_merge_kernel, alpha=alpha),
        grid=(tokens // tm,),
        in_specs=[tok, tok, tok, tok, tok, _resident(w4.shape), _resident(g_a.shape), _resident(g_b.shape),
                  _resident(wo.shape), _resident(ln_g.shape), _resident(ln_b.shape)],
        out_specs=tok,
        out_shape=jax.ShapeDtypeStruct((tokens, D_MODEL), F32),
        scratch_shapes=[pltpu.VMEM((tm, V_W), BF16)],
        compiler_params=pltpu.CompilerParams(dimension_semantics=("parallel",),
                                             vmem_limit_bytes=VMEM_LIMIT),
        name="merge",
    )(x2d, flat(hf), flat(hb), flat(rf), flat(rb), w4, g_a, g_b, wo, ln_g, ln_b)


def _ffn_kernel(x_ref, xp_ref, xn_ref, wup_ref, cw_ref, cb_ref, wdn_ref, lng_ref, lnb_ref, o_ref,
                acc_ref, hv_ref, hg_ref, *, alpha, tiles_per_seq):
    tm = FFN_TM
    rows = tm + 2 * HALO
    n_slabs = FFN_TF // LANES
    pos = lax.rem(pl.program_id(0), tiles_per_seq)
    x = x_ref[...]
    x_prev = jnp.where(pos == 0, 0.0, xp_ref[...])
    x_next = jnp.where(pos == tiles_per_seq - 1, 0.0, xn_ref[...])
    xe = jnp.concatenate([x_prev, x, x_next], axis=0).astype(BF16)

    def put(h_ref, slot, hu):
        for c in range(n_slabs):
            h_ref[slot, c, pl.ds(0, rows, stride=2), :] = hu[:, c * LANES:(c + 1) * LANES]

    def up(j):
        c_val = j * FFN_TF
        c_gate = D_FF + j * FFN_TF
        put(hv_ref, j % 2, _dot(xe, wup_ref[:, c_val:c_val + FFN_TF]))
        put(hg_ref, j % 2, _dot(xe, wup_ref[:, c_gate:c_gate + FFN_TF]))

    def conv(h_ref, slot, c0):
        out = []
        for c in range(n_slabs):
            cols = slice(c0 + c * LANES, c0 + (c + 1) * LANES)
            w = cw_ref[:, cols]
            taps = [h_ref[slot, c, pl.ds(2 * (HALO - 1 + k), tm, stride=2), :] for k in range(3)]
            out.append(w[0:1] * taps[0] + w[1:2] * taps[1] + w[2:3] * taps[2] + cb_ref[:, cols])
        return jnp.concatenate(out, axis=1)

    n_chunks = D_FF // FFN_TF
    up(0)
    for j in range(n_chunks):
        if j + 1 < n_chunks:
            up(j + 1)
        c_val = j * FFN_TF
        f = (jax.nn.gelu(conv(hg_ref, j % 2, D_FF + c_val)) * conv(hv_ref, j % 2, c_val)).astype(BF16)
        part = _dot(f, wdn_ref[c_val:c_val + FFN_TF, :])
        if j == 0:
            acc_ref[...] = part
        else:
            acc_ref[...] += part
    o_ref[...] = _layer_norm(alpha * x + acc_ref[...], lng_ref[...], lnb_ref[...])


def _ffn_call(x2d, seq, wup, cw, cb, wdn, ln_g, ln_b, alpha):
    tokens = x2d.shape[0]
    tm = FFN_TM
    per = tm // HALO
    last = tokens // HALO - 1
    tok = pl.BlockSpec((tm, D_MODEL), lambda i: (i, 0))
    return pl.pallas_call(
        functools.partial(_ffn_kernel, alpha=alpha, tiles_per_seq=seq // tm),
        grid=(tokens // tm,),
        in_specs=[tok,
                  pl.BlockSpec((HALO, D_MODEL), lambda i: (jnp.maximum(i * per - 1, 0), 0)),
                  pl.BlockSpec((HALO, D_MODEL), lambda i: (jnp.minimum((i + 1) * per, last), 0)),
                  _resident(wup.shape), _resident(cw.shape), _resident(cb.shape), _resident(wdn.shape),
                  _resident(ln_g.shape), _resident(ln_b.shape)],
        out_specs=tok,
        out_shape=jax.ShapeDtypeStruct((tokens, D_MODEL), F32),
        scratch_shapes=[pltpu.VMEM((tm, D_MODEL), F32),
                        pltpu.VMEM((2, FFN_TF // LANES, 2 * (tm + 2 * HALO), LANES), F32),
                        pltpu.VMEM((2, FFN_TF // LANES, 2 * (tm + 2 * HALO), LANES), F32)],
        compiler_params=pltpu.CompilerParams(dimension_semantics=("parallel",),
                                             vmem_limit_bytes=VMEM_LIMIT),
        name="ffn",
    )(x2d, x2d, x2d, wup, cw, cb, wdn, ln_g, ln_b)


def _rope_tables(seq):
    inv_freq = ROPE_BASE ** (-jnp.arange(0, B_QK, 2, dtype=F32) / B_QK)
    ang = jnp.arange(seq, dtype=F32)[:, None] * inv_freq[None, :]
    cos = jnp.cos(ang)
    sin = jnp.sin(ang)
    return cos, sin, cos.T, sin.T


def _layer_weights(l, w_in, b_if, w_o, w_up, w_down):
    w = w_in[l]
    cols = lambda o, n: w[:, o:o + n]
    wtok = jnp.concatenate([cols(O_QA, A_QK_W), cols(O_VA, V_W), cols(O_QB, B_QK_W), cols(O_VB, V_W)],
                           axis=1).astype(BF16)
    wt = jnp.concatenate([cols(O_KA, A_QK_W), cols(O_KB, B_QK_W)], axis=1).T.astype(BF16)
    wg = cols(O_G, N_GATES).T.astype(BF16)
    bg = b_if[l].astype(F32).reshape(N_GATES, 1)
    w4 = jnp.concatenate([cols(O_OA, V_W), cols(O_GB, V_W), cols(O_MA, D_MODEL), cols(O_MB, D_MODEL)],
                         axis=1).astype(BF16)
    return wtok, wt, wg, bg, w4, w_o[l].astype(BF16), w_up[l].astype(BF16), w_down[l].astype(BF16)


def _trunk(x, ln_in, layers, alpha):
    batch, seq, _ = x.shape
    assert seq % PROJ_TM == 0 and seq % FFN_TM == 0 and seq % MIX_L == 0
    rope = _rope_tables(seq)
    x2d = x.reshape(batch * seq, D_MODEL)
    for l, lw in enumerate(layers):
        (wtok, wt, wg, bg, w4, wo, wup, wdn, lg, g_a, g_b, ln1, cw, cb, ln2) = lw
        outs = _proj_call(x2d, batch, seq, ln_in if l == 0 else None, wtok, wt, wg, bg, rope)
        qa, va, qb, vb, kt, gt = outs[:6]
        if l == 0:
            x2d = outs[6]
        hf, hb, rf, rb = _mixer_call(lg, qa, va, qb, vb, kt, _gates_call(gt), batch, seq)
        x2d = _merge_call(x2d, hf, hb, rf, rb, w4, g_a, g_b, wo, ln1[0], ln1[1], alpha)
        x2d = _ffn_call(x2d, seq, wup, cw, cb, wdn, ln2[0], ln2[1], alpha)
    return x2d.reshape(batch, seq, D_MODEL)


def kernel(x_prompt, x_sample, ln_in_g, ln_in_b, w_in, b_if, ret_log_decay, mlstm_norm_g, ret_norm_g, w_o, ln1_g, ln1_b, w_up, conv_w, conv_b, w_down, ln2_g, ln2_b):
    depth = w_in.shape[0]
    alpha = (2 * depth) ** 0.25
    row = lambda a: a.astype(F32).reshape(1, -1)
    layers = []
    for l in range(depth):
        layers.append(_layer_weights(l, w_in, b_if, w_o, w_up, w_down) + (
            ret_log_decay[l].astype(F32), row(mlstm_norm_g[l]), row(ret_norm_g[l]),
            (row(ln1_g[l]), row(ln1_b[l])), conv_w[l].astype(F32), row(conv_b[l]),
            (row(ln2_g[l]), row(ln2_b[l]))))
    ln_in = (row(ln_in_g), row(ln_in_b))
    return (_trunk(x_prompt, ln_in, layers, alpha), _trunk(x_sample, ln_in, layers, alpha))
```

```python
import functools

import jax
import jax.numpy as jnp
from jax import lax
from jax.experimental import pallas as pl
from jax.experimental.pallas import tpu as pltpu

F32 = jnp.float32
BF16 = jnp.bfloat16

D_MODEL = 1024
HEADS = 4
A_QK = 128
HEAD_V = 256
B_QK = 256
D_FF = 2816
MIX_L = 256
LANES = 128
LOG2E = 1.4426950408889634
LN_EPS = 1e-5
ROPE_BASE = 10000.0

A_QK_W = HEADS * A_QK
V_W = HEADS * HEAD_V
B_QK_W = HEADS * B_QK
N_GATES = 16
N_GROWS = 24
KT_ROWS = A_QK_W + B_QK_W

_SPLITS = (A_QK_W, A_QK_W, V_W, V_W, N_GATES, B_QK_W, B_QK_W, V_W, V_W, D_MODEL, D_MODEL)
_OFF = [0]
for _w in _SPLITS:
    _OFF.append(_OFF[-1] + _w)
(O_QA, O_KA, O_VA, O_OA, O_G, O_QB, O_KB, O_VB, O_GB, O_MA, O_MB, _) = _OFF

PROJ_TM = 512
MERGE_TM = 512
FFN_TM = 512
FFN_TF = 256
FFN_SLOTS = 4
HALO = 8
VMEM_LIMIT = 56 * 1024 * 1024

_NT = (((1,), (1,)), ((), ()))


def _dot(a, b):
    return jnp.dot(a, b, preferred_element_type=F32)


def _dot_nt(a, b):
    return lax.dot_general(a, b, _NT, preferred_element_type=F32)


def _layer_norm(x, g, b):
    mu = jnp.mean(x, axis=-1, keepdims=True)
    xc = x - mu
    var = jnp.mean(xc * xc, axis=-1, keepdims=True)
    return xc * lax.rsqrt(var + LN_EPS) * g + b


def _resident(shape):
    nd = len(shape)
    return pl.BlockSpec(shape, lambda *_: (0,) * nd, pipeline_mode=pl.Buffered(1))


def _scan_chunks(x, op, identity, reverse):
    per_chunk = MIX_L // LANES
    blocks = [x[:, j * LANES:(j + 1) * LANES] for j in range(x.shape[1] // LANES)]
    lane = lax.broadcasted_iota(jnp.int32, blocks[0].shape, 1)
    for j, y in enumerate(blocks):
        k = 1
        while k < LANES:
            if reverse:
                y = op(y, jnp.where(lane < LANES - k, pltpu.roll(y, LANES - k, axis=1), identity))
            else:
                y = op(y, jnp.where(lane >= k, pltpu.roll(y, k, axis=1), identity))
            k *= 2
        blocks[j] = y
    for c0 in range(0, len(blocks), per_chunk):
        if reverse:
            for j in range(c0 + per_chunk - 2, c0 - 1, -1):
                blocks[j] = op(blocks[j], blocks[j + 1][:, 0:1])
        else:
            for j in range(c0 + 1, c0 + per_chunk):
                blocks[j] = op(blocks[j], blocks[j - 1][:, LANES - 1:LANES])
    return jnp.concatenate(blocks, axis=1)


def _proj_kernel(*refs, apply_ln):
    if apply_ln:
        (x_ref, lng_ref, lnb_ref, wtok_ref, wt_ref, wg_ref, bg_ref, cos_ref, sin_ref, cost_ref, sint_ref,
         qa_ref, va_ref, qb_ref, vb_ref, kt_ref, gt_ref, xn_ref) = refs
        x = _layer_norm(x_ref[...], lng_ref[...], lnb_ref[...])
        xn_ref[...] = x
    else:
        (x_ref, wtok_ref, wt_ref, wg_ref, bg_ref, cos_ref, sin_ref, cost_ref, sint_ref,
         qa_ref, va_ref, qb_ref, vb_ref, kt_ref, gt_ref) = refs
        x = x_ref[...]
    xb = x.astype(BF16)

    qa_ref[...] = (_dot(xb, wtok_ref[:, 0:A_QK_W]) * (A_QK ** -0.5)).astype(BF16)
    half = V_W // 2
    for j in range(2):
        c0 = A_QK_W + j * half
        va_ref[:, j * half:(j + 1) * half] = _dot(xb, wtok_ref[:, c0:c0 + half]).astype(BF16)
    cos = cos_ref[...]
    sin = sin_ref[...]
    hq = B_QK // 2
    for h in range(HEADS):
        c0 = A_QK_W + V_W + h * B_QK
        t = _dot(xb, wtok_ref[:, c0:c0 + B_QK])
        t1 = t[:, :hq]
        t2 = t[:, hq:]
        qb_ref[:, h * B_QK:h * B_QK + hq] = (t1 * cos - t2 * sin).astype(BF16)
        qb_ref[:, h * B_QK + hq:(h + 1) * B_QK] = (t1 * sin + t2 * cos).astype(BF16)
    for j in range(2):
        c0 = A_QK_W + V_W + B_QK_W + j * half
        vb_ref[:, j * half:(j + 1) * half] = _dot(xb, wtok_ref[:, c0:c0 + half]).astype(BF16)

    for j in range(2):
        r0 = j * 256
        kt_ref[r0:r0 + 256, :] = _dot_nt(wt_ref[r0:r0 + 256, :], xb).astype(BF16)
    cost = cost_ref[...]
    sint = sint_ref[...]
    for h in range(HEADS):
        r0 = A_QK_W + h * B_QK
        t = _dot_nt(wt_ref[r0:r0 + B_QK, :], xb)
        t1 = t[:hq, :]
        t2 = t[hq:, :]
        kt_ref[r0:r0 + hq, :] = ((t1 * cost - t2 * sint) * (B_QK ** -0.5)).astype(BF16)
        kt_ref[r0 + hq:r0 + B_QK, :] = ((t1 * sint + t2 * cost) * (B_QK ** -0.5)).astype(BF16)

    g = _dot_nt(wg_ref[...], xb) + bg_ref[...]
    row = lax.broadcasted_iota(jnp.int32, g.shape, 0)
    log_sig = jnp.minimum(g, 0.0) - jnp.log1p(jnp.exp(-jnp.abs(g)))
    gt_ref[...] = jnp.where((row & HEADS) != 0, log_sig, g)


def _gates_kernel(g_ref, o_ref):
    g = g_ref[...]
    first4 = lax.broadcasted_iota(jnp.int32, (8, g.shape[1]), 0) < HEADS
    r_pair, cb_pair, cmr_pair = [], [], []
    for d in range(2):
        gd = g[d * 8:(d + 1) * 8, :]
        cums = _scan_chunks(gd, jnp.add, 0.0, reverse=(d == 1))
        r = gd - pltpu.roll(cums, HEADS, axis=0)
        r_pair.append(r)
        cb_pair.append(cums)
        cmr_pair.append(_scan_chunks(r, jnp.maximum, -jnp.inf, reverse=(d == 1)))
    o_ref[0:8, :] = jnp.where(first4, r_pair[0], pltpu.roll(r_pair[1], HEADS, axis=0))
    o_ref[8:16, :] = jnp.where(first4, pltpu.roll(cb_pair[0], HEADS, axis=0), cb_pair[1])
    o_ref[16:24, :] = jnp.where(first4, cmr_pair[0], pltpu.roll(cmr_pair[1], HEADS, axis=0))


def _gates_call(gt):
    batch, _, seq = gt.shape
    return pl.pallas_call(
        _gates_kernel,
        grid=(batch,),
        in_specs=[pl.BlockSpec((None, N_GATES, seq), lambda b: (b, 0, 0))],
        out_specs=pl.BlockSpec((None, N_GROWS, seq), lambda b: (b, 0, 0)),
        out_shape=jax.ShapeDtypeStruct((batch, N_GROWS, seq), F32),
        compiler_params=pltpu.CompilerParams(dimension_semantics=("parallel",),
                                             vmem_limit_bytes=VMEM_LIMIT),
        name="gates",
    )(gt)


def _proj_call(x2d, batch, seq, ln, wtok, wt, wg, bg, rope):
    tokens = x2d.shape[0]
    tm = PROJ_TM
    nt = seq // tm
    cos, sin, cost, sint = rope
    tok = lambda w: pl.BlockSpec((tm, w), lambda i: (i, 0))
    in_specs = [tok(D_MODEL)]
    args = [x2d]
    if ln is not None:
        in_specs += [_resident((1, D_MODEL)), _resident((1, D_MODEL))]
        args += list(ln)
    in_specs += [_resident(wtok.shape), _resident(wt.shape), _resident(wg.shape), _resident(bg.shape),
                 pl.BlockSpec((tm, B_QK // 2), lambda i: (i % nt, 0)),
                 pl.BlockSpec((tm, B_QK // 2), lambda i: (i % nt, 0)),
                 pl.BlockSpec((B_QK // 2, tm), lambda i: (0, i % nt)),
                 pl.BlockSpec((B_QK // 2, tm), lambda i: (0, i % nt))]
    args += [wtok, wt, wg, bg, cos, sin, cost, sint]
    out_shape = [jax.ShapeDtypeStruct((tokens, A_QK_W), BF16),
                 jax.ShapeDtypeStruct((tokens, V_W), BF16),
                 jax.ShapeDtypeStruct((tokens, B_QK_W), BF16),
                 jax.ShapeDtypeStruct((tokens, V_W), BF16),
                 jax.ShapeDtypeStruct((batch, KT_ROWS, seq), BF16),
                 jax.ShapeDtypeStruct((batch, N_GATES, seq), F32)]
    out_specs = [tok(A_QK_W), tok(V_W), tok(B_QK_W), tok(V_W),
                 pl.BlockSpec((None, KT_ROWS, tm), lambda i: (i // nt, 0, i % nt)),
                 pl.BlockSpec((None, N_GATES, tm), lambda i: (i // nt, 0, i % nt))]
    if ln is not None:
        out_shape.append(jax.ShapeDtypeStruct((tokens, D_MODEL), F32))
        out_specs.append(tok(D_MODEL))
    return pl.pallas_call(
        functools.partial(_proj_kernel, apply_ln=ln is not None),
        grid=(tokens // tm,),
        in_specs=in_specs,
        out_specs=out_specs,
        out_shape=out_shape,
        compiler_params=pltpu.CompilerParams(dimension_semantics=("parallel",),
                                             vmem_limit_bytes=VMEM_LIMIT),
        name="proj",
    )(*args)


def _mixer_kernel(lg_ref,
                  qa_f, va_f, qb_f, vb_f, kt_f, g_f,
                  qa_b, va_b, qb_b, vb_b, kt_b, g_b,
                  hf_ref, hb_ref, rf_ref, rb_ref,
                  c_ref, nv_ref, m_ref, r_ref, dm_ref, wi_ref, rc_ref):
    L = MIX_L
    t_i = lax.broadcasted_iota(jnp.int32, (L, L), 0)
    s_i = lax.broadcasted_iota(jnp.int32, (L, L), 1)

    @pl.when(pl.program_id(1) == 0)
    def _init():
        c_ref[...] = jnp.zeros_like(c_ref)
        nv_ref[...] = jnp.zeros_like(nv_ref)
        m_ref[...] = jnp.zeros_like(m_ref)
        r_ref[...] = jnp.zeros_like(r_ref)
        tcol = lax.broadcasted_iota(jnp.int32, (L, LANES), 0).astype(F32)
        srow = lax.broadcasted_iota(jnp.int32, (8, L), 1).astype(F32)
        row8 = lax.broadcasted_iota(jnp.int32, (8, L), 0)
        for d in range(2):
            diff = ((t_i - s_i) if d == 0 else (s_i - t_i)).astype(F32)
            for h in range(HEADS):
                lg = lg_ref[d, h]
                dm_ref[d * HEADS + h] = jnp.where(diff >= 0.0, jnp.exp(lg * jnp.maximum(diff, 0.0)), 0.0)
                if d == 0:
                    wi = jnp.exp(lg * (tcol + 1.0))
                    ws = jnp.exp(lg * (L - 1.0 - srow))
                else:
                    wi = jnp.exp(lg * (L - tcol))
                    ws = jnp.exp(lg * srow)
                wi_ref[d * HEADS + h] = wi
                chunk_dec = jnp.exp(jnp.full((8, L), lg * float(L), F32))
                rc_ref[d * HEADS + h] = jnp.where(row8 == 0, ws, chunk_dec)

    eye = (lax.broadcasted_iota(jnp.int32, (A_QK, LANES), 0)
           == lax.broadcasted_iota(jnp.int32, (A_QK, LANES), 1))
    dirs = ((qa_f, va_f, qb_f, vb_f, kt_f, g_f, hf_ref, rf_ref),
            (qa_b, va_b, qb_b, vb_b, kt_b, g_b, hb_ref, rb_ref))
    bodies = [(d, h) for d in range(2) for h in range(HEADS)]

    def operands(d, h):
        qa, va, qb, vb, kt = dirs[d][:5]
        return dict(
            q_a=qa[:, h * A_QK:(h + 1) * A_QK],
            kt_a=kt[h * A_QK:(h + 1) * A_QK, :],
            v_a=va[:, h * HEAD_V:(h + 1) * HEAD_V],
            q_b=qb[:, h * B_QK:(h + 1) * B_QK],
            kt_b=kt[A_QK_W + h * B_QK:A_QK_W + (h + 1) * B_QK, :],
            v_b=vb[:, h * HEAD_V:(h + 1) * HEAD_V])


    prep = {}
    for d, h in bodies:
        idx = d * HEADS + h
        g = dirs[d][5]
        end = 0 if d == 1 else L - 1
        r_row = g[idx:idx + 1, :]
        cb_row = g[8 + idx:8 + idx + 1, :]
        cmr_row = g[16 + idx:16 + idx + 1, :]
        cb_col = jnp.transpose(cb_row)
        cmr_col = jnp.transpose(cmr_row)
        m_old = m_ref[idx][0:1, 0:1]
        big_m = jnp.maximum(m_old, cmr_col)
        m_top = jnp.maximum(m_old, cmr_row[:, end:end + 1])
        prep[(d, h)] = dict(
            r2_row=r_row * LOG2E, big_m2=big_m * LOG2E,
            w_inter=jnp.exp(m_old - big_m), floor=jnp.exp(-(cb_col + big_m)),
            w_s=jnp.exp(r_row - m_top), dec=jnp.exp(m_old - m_top),
            m_new=cb_row[:, end:end + 1] + m_top)

    half = L // 2
    t_h = lax.broadcasted_iota(jnp.int32, (half, half), 0)
    s_h = lax.broadcasted_iota(jnp.int32, (half, half), 1)
    st2 = {}
    for d, h in bodies:
        idx = d * HEADS + h
        o = operands(d, h)
        pr = prep[(d, h)]
        tri = (s_h >= t_h) if d == 1 else (s_h <= t_h)
        nr = slice(half, L) if d == 1 else slice(0, half)
        wr = slice(0, half) if d == 1 else slice(half, L)
        r2, bm = pr["r2_row"], pr["big_m2"]
        e_n = jnp.where(tri, jnp.exp2(r2[:, nr] - bm[nr]), 0.0)
        e_w_diag = jnp.where(tri, jnp.exp2(r2[:, wr] - bm[wr]), 0.0)
        e_w_full = jnp.exp2(r2[:, nr] - bm[wr])
        e_w = jnp.concatenate([e_w_diag, e_w_full] if d == 1 else [e_w_full, e_w_diag], axis=1)
        p_n = _dot(o["q_a"][nr], o["kt_a"][:, nr]) * e_n
        p_w = _dot(o["q_a"][wr], o["kt_a"]) * e_w
        dm = dm_ref[idx]
        a_n = _dot(o["q_b"][nr], o["kt_b"][:, nr]) * dm[nr, nr]
        a_w = _dot(o["q_b"][wr], o["kt_b"]) * dm[wr, :]
        kw_t = o["kt_a"].astype(F32) * pr["w_s"]
        kbw_t = o["kt_b"].astype(F32) * rc_ref[idx][0:1, :]
        st2[(d, h)] = dict(
            blocks=((nr, nr, p_n.astype(BF16), jnp.sum(p_n, axis=1, keepdims=True), a_n.astype(BF16)),
                    (wr, slice(0, L), p_w.astype(BF16), jnp.sum(p_w, axis=1, keepdims=True), a_w.astype(BF16))),
            kw=kw_t.astype(BF16), kwsum=jnp.sum(kw_t, axis=1, keepdims=True), kbw=kbw_t.astype(BF16))

    for d, h in bodies:
        idx = d * HEADS + h
        o = operands(d, h)
        pr, s2 = prep[(d, h)], st2[(d, h)]
        h_out, r_out = dirs[d][6], dirs[d][7]
        c_old = c_ref[idx]
        r_old = r_ref[idx]
        nv_old = nv_ref[idx]
        wi = wi_ref[idx]
        wi2 = jnp.concatenate([wi, wi], axis=1)
        nv_row = jnp.sum(jnp.where(eye, nv_old, 0.0), axis=0, keepdims=True)
        q_nv = jnp.sum(o["q_a"].astype(F32) * nv_row, axis=1, keepdims=True)
        qc = _dot(o["q_a"], c_old.astype(BF16))
        qr = _dot(o["q_b"], r_old.astype(BF16))
        for rows, cols, pb, psum, ab in s2["blocks"]:
            den = psum + pr["w_inter"][rows] * q_nv[rows]
            inv = 1.0 / jnp.maximum(jnp.abs(den), pr["floor"][rows])
            num = _dot(pb, o["v_a"][cols]) + pr["w_inter"][rows] * qc[rows]
            h_out[rows, h * HEAD_V:(h + 1) * HEAD_V] = num * inv
            r_out[rows, h * HEAD_V:(h + 1) * HEAD_V] = _dot(ab, o["v_b"][cols]) + wi2[rows] * qr[rows]
        c_ref[idx] = pr["dec"] * c_old + _dot(s2["kw"], o["v_a"])
        nv_ref[idx] = pr["dec"] * nv_old + s2["kwsum"]
        m_ref[idx] = jnp.broadcast_to(pr["m_new"], m_ref.shape[1:])
        r_ref[idx] = rc_ref[idx][1:2, 0:1] * r_old + _dot(s2["kbw"], o["v_b"])


def _mixer_call(lg, qa, va, qb, vb, kt, gt, batch, seq):
    L = MIX_L
    n = seq // L
    qa, va, qb, vb = (a.reshape(batch, seq, a.shape[-1]) for a in (qa, va, qb, vb))

    def specs(chunk_of):
        tokm = lambda w: pl.BlockSpec((None, L, w), lambda b, c: (b, chunk_of(c), 0))
        featm = lambda r: pl.BlockSpec((None, r, L), lambda b, c: (b, 0, chunk_of(c)))
        return [tokm(A_QK_W), tokm(V_W), tokm(B_QK_W), tokm(V_W), featm(KT_ROWS), featm(N_GROWS)]

    fwd = lambda c: c
    bwd = lambda c: n - 1 - c
    out_f = pl.BlockSpec((None, L, V_W), lambda b, c: (b, c, 0))
    out_b = pl.BlockSpec((None, L, V_W), lambda b, c: (b, n - 1 - c, 0))
    out_sds = jax.ShapeDtypeStruct((batch, seq, V_W), F32)
    nstate = 2 * HEADS
    return pl.pallas_call(
        _mixer_kernel,
        grid=(batch, n),
        in_specs=[pl.BlockSpec(memory_space=pltpu.SMEM)] + specs(fwd) + specs(bwd),
        out_specs=[out_f, out_b, out_f, out_b],
        out_shape=[out_sds] * 4,
        scratch_shapes=[pltpu.VMEM((nstate, A_QK, HEAD_V), F32),
                        pltpu.VMEM((nstate, A_QK, LANES), F32),
                        pltpu.VMEM((nstate, 8, LANES), F32),
                        pltpu.VMEM((nstate, B_QK, HEAD_V), F32),
                        pltpu.VMEM((nstate, L, L), F32),
                        pltpu.VMEM((nstate, L, LANES), F32),
                        pltpu.VMEM((nstate, 8, L), F32)],
        compiler_params=pltpu.CompilerParams(dimension_semantics=("parallel", "arbitrary"),
                                             vmem_limit_bytes=VMEM_LIMIT),
        name="mixer",
    )(lg, qa, va, qb, vb, kt, gt, qa, va, qb, vb, kt, gt)


def _head_norm(h):
    mu = jnp.mean(h, axis=-1, keepdims=True)
    hc = h - mu
    var = jnp.mean(hc * hc, axis=-1, keepdims=True)
    return hc * lax.rsqrt(var + LN_EPS)


def _merge_kernel(x_ref, hf_ref, hb_ref, rf_ref, rb_ref, w4_ref, ga_ref, gb_ref, wo_ref, lng_ref, lnb_ref,
                  o_ref, merged_ref, *, alpha):
    x = x_ref[...]
    xb = x.astype(BF16)
    for h in range(HEADS):
        sl = slice(h * HEAD_V, (h + 1) * HEAD_V)
        col = lambda piece: slice(piece * V_W + h * HEAD_V, piece * V_W + (h + 1) * HEAD_V)
        o_gate = jax.nn.sigmoid(_dot(xb, w4_ref[:, col(0)]))
        y_a = _head_norm((hf_ref[:, sl] + hb_ref[:, sl]) * o_gate) * ga_ref[:, sl]
        swish = jax.nn.silu(_dot(xb, w4_ref[:, col(1)]))
        y_b = _head_norm(rf_ref[:, sl] + rb_ref[:, sl]) * gb_ref[:, sl] * swish
        m_a = jax.nn.sigmoid(_dot(xb, w4_ref[:, col(2)]))
        m_b = jax.nn.sigmoid(_dot(xb, w4_ref[:, col(3)]))
        merged_ref[:, sl] = (m_a * y_a + m_b * y_b).astype(BF16)
    z = alpha * x + _dot(merged_ref[...], wo_ref[...])
    o_ref[...] = _layer_norm(z, lng_ref[...], lnb_ref[...])


def _merge_call(x2d, hf, hb, rf, rb, w4, g_a, g_b, wo, ln_g, ln_b, alpha):
    tokens = x2d.shape[0]
    tm = MERGE_TM
    tok = pl.BlockSpec((tm, D_MODEL), lambda i: (i, 0))
    flat = lambda a: a.reshape(tokens, V_W)
    return pl.pallas_call(
        functools.partial(_merge_kernel, alpha=alpha),
        grid=(tokens // tm,),
        in_specs=[tok, tok, tok, tok, tok, _resident(w4.shape), _resident(g_a.shape), _resident(g_b.shape),
                  _resident(wo.shape), _resident(ln_g.shape), _resident(ln_b.shape)],
        out_specs=tok,
        out_shape=jax.ShapeDtypeStruct((tokens, D_MODEL), F32),
        scratch_shapes=[pltpu.VMEM((tm, V_W), BF16)],
        compiler_params=pltpu.CompilerParams(dimension_semantics=("parallel",),
                                             vmem_limit_bytes=VMEM_LIMIT),
        name="merge",
    )(x2d, flat(hf), flat(hb), flat(rf), flat(rb), w4, g_a, g_b, wo, ln_g, ln_b)


def _ffn_kernel(x_ref, xp_ref, xn_ref, wup_ref, cw_ref, cb_ref, wdn_ref, lng_ref, lnb_ref, o_ref,
                acc_ref, hv_ref, hg_ref, *, alpha, tiles_per_seq):
    tm = FFN_TM
    rows = tm + 2 * HALO
    n_slabs = FFN_TF // LANES
    pos = lax.rem(pl.program_id(0), tiles_per_seq)
    x = x_ref[...]
    x_prev = jnp.where(pos == 0, 0.0, xp_ref[...])
    x_next = jnp.where(pos == tiles_per_seq - 1, 0.0, xn_ref[...])
    xe = jnp.concatenate([x_prev, x, x_next], axis=0).astype(BF16)

    def put(h_ref, slot, hu):
        for c in range(n_slabs):
            h_ref[slot, c, pl.ds(0, rows, stride=2), :] = hu[:, c * LANES:(c + 1) * LANES]

    def up(j):
        c_val = j * FFN_TF
        c_gate = D_FF + j * FFN_TF
        put(hv_ref, j % FFN_SLOTS, _dot(xe, wup_ref[:, c_val:c_val + FFN_TF]))
        put(hg_ref, j % FFN_SLOTS, _dot(xe, wup_ref[:, c_gate:c_gate + FFN_TF]))

    def conv(h_ref, slot, c0):
        out = []
        for c in range(n_slabs):
            cols = slice(c0 + c * LANES, c0 + (c + 1) * LANES)
            w = cw_ref[:, cols]
            taps = [h_ref[slot, c, pl.ds(2 * (HALO - 1 + k), tm, stride=2), :] for k in range(3)]
            out.append(w[0:1] * taps[0] + w[1:2] * taps[1] + w[2:3] * taps[2] + cb_ref[:, cols])
        return jnp.concatenate(out, axis=1)

    n_chunks = D_FF // FFN_TF
    for j in range(FFN_SLOTS - 1):
        up(j)
    for j in range(n_chunks):
        if j + FFN_SLOTS - 1 < n_chunks:
            up(j + FFN_SLOTS - 1)
        c_val = j * FFN_TF
        slot = j % FFN_SLOTS
        f = (jax.nn.gelu(conv(hg_ref, slot, D_FF + c_val)) * conv(hv_ref, slot, c_val)).astype(BF16)
        part = _dot(f, wdn_ref[c_val:c_val + FFN_TF, :])
        if j == 0:
            acc_ref[...] = part
        else:
            acc_ref[...] += part
    o_ref[...] = _layer_norm(alpha * x + acc_ref[...], lng_ref[...], lnb_ref[...])


def _ffn_call(x2d, seq, wup, cw, cb, wdn, ln_g, ln_b, alpha):
    tokens = x2d.shape[0]
    tm = FFN_TM
    per = tm // HALO
    last = tokens // HALO - 1
    tok = pl.BlockSpec((tm, D_MODEL), lambda i: (i, 0))
    return pl.pallas_call(
        functools.partial(_ffn_kernel, alpha=alpha, tiles_per_seq=seq // tm),
        grid=(tokens // tm,),
        in_specs=[tok,
                  pl.BlockSpec((HALO, D_MODEL), lambda i: (jnp.maximum(i * per - 1, 0), 0)),
                  pl.BlockSpec((HALO, D_MODEL), lambda i: (jnp.minimum((i + 1) * per, last), 0)),
                  _resident(wup.shape), _resident(cw.shape), _resident(cb.shape), _resident(wdn.shape),
                  _resident(ln_g.shape), _resident(ln_b.shape)],
        out_specs=tok,
        out_shape=jax.ShapeDtypeStruct((tokens, D_MODEL), F32),
        scratch_shapes=[pltpu.VMEM((tm, D_MODEL), F32),
                        pltpu.VMEM((FFN_SLOTS, FFN_TF // LANES, 2 * (tm + 2 * HALO), LANES), F32),
                        pltpu.VMEM((FFN_SLOTS, FFN_TF // LANES, 2 * (tm + 2 * HALO), LANES), F32)],
        compiler_params=pltpu.CompilerParams(dimension_semantics=("parallel",),
                                             vmem_limit_bytes=VMEM_LIMIT),
        name="ffn",
    )(x2d, x2d, x2d, wup, cw, cb, wdn, ln_g, ln_b)


def _rope_tables(seq):
    inv_freq = ROPE_BASE ** (-jnp.arange(0, B_QK, 2, dtype=F32) / B_QK)
    ang = jnp.arange(seq, dtype=F32)[:, None] * inv_freq[None, :]
    cos = jnp.cos(ang)
    sin = jnp.sin(ang)
    return cos, sin, cos.T, sin.T


def _layer_weights(l, w_in, b_if, w_o, w_up, w_down):
    w = w_in[l]
    cols = lambda o, n: w[:, o:o + n]
    wtok = jnp.concatenate([cols(O_QA, A_QK_W), cols(O_VA, V_W), cols(O_QB, B_QK_W), cols(O_VB, V_W)],
                           axis=1).astype(BF16)
    wt = jnp.concatenate([cols(O_KA, A_QK_W), cols(O_KB, B_QK_W)], axis=1).T.astype(BF16)
    wg = cols(O_G, N_GATES).T.astype(BF16)
    bg = b_if[l].astype(F32).reshape(N_GATES, 1)
    w4 = jnp.concatenate([cols(O_OA, V_W), cols(O_GB, V_W), cols(O_MA, D_MODEL), cols(O_MB, D_MODEL)],
                         axis=1).astype(BF16)
    return wtok, wt, wg, bg, w4, w_o[l].astype(BF16), w_up[l].astype(BF16), w_down[l].astype(BF16)


def _trunk(x, ln_in, layers, alpha):
    batch, seq, _ = x.shape
    assert seq % PROJ_TM == 0 and seq % FFN_TM == 0 and seq % MIX_L == 0
    rope = _rope_tables(seq)
    x2d = x.reshape(batch * seq, D_MODEL)
    for l, lw in enumerate(layers):
        (wtok, wt, wg, bg, w4, wo, wup, wdn, lg, g_a, g_b, ln1, cw, cb, ln2) = lw
        outs = _proj_call(x2d, batch, seq, ln_in if l == 0 else None, wtok, wt, wg, bg, rope)
        qa, va, qb, vb, kt, gt = outs[:6]
        if l == 0:
            x2d = outs[6]
        hf, hb, rf, rb = _mixer_call(lg, qa, va, qb, vb, kt, _gates_call(gt), batch, seq)
        x2d = _merge_call(x2d, hf, hb, rf, rb, w4, g_a, g_b, wo, ln1[0], ln1[1], alpha)
        x2d = _ffn_call(x2d, seq, wup, cw, cb, wdn, ln2[0], ln2[1], alpha)
    return x2d.reshape(batch, seq, D_MODEL)


def kernel(x_prompt, x_sample, ln_in_g, ln_in_b, w_in, b_if, ret_log_decay, mlstm_norm_g, ret_norm_g, w_o, ln1_g, ln1_b, w_up, conv_w, conv_b, w_down, ln2_g, ln2_b):
    depth = w_in.shape[0]
    alpha = (2 * depth) ** 0.25
    row = lambda a: a.astype(F32).reshape(1, -1)
    layers = []
    for l in range(depth):
        layers.append(_layer_weights(l, w_in, b_if, w_o, w_up, w_down) + (
            ret_log_decay[l].astype(F32), row(mlstm_norm_g[l]), row(ret_norm_g[l]),
            (row(ln1_g[l]), row(ln1_b[l])), conv_w[l].astype(F32), row(conv_b[l]),
            (row(ln2_g[l]), row(ln2_b[l]))))
    ln_in = (row(ln_in_g), row(ln_in_b))
    return (_trunk(x_prompt, ln_in, layers, alpha), _trunk(x_sample, ln_in, layers, alpha))
```

```python
import functools

import jax
import jax.numpy as jnp
from jax import lax
from jax.experimental import pallas as pl
from jax.experimental.pallas import tpu as pltpu

F32 = jnp.float32
BF16 = jnp.bfloat16

D_MODEL = 1024
HEADS = 4
A_QK = 128
HEAD_V = 256
B_QK = 256
D_FF = 2816
MIX_L = 256
LANES = 128
LOG2E = 1.4426950408889634
LN_EPS = 1e-5
ROPE_BASE = 10000.0

A_QK_W = HEADS * A_QK
V_W = HEADS * HEAD_V
B_QK_W = HEADS * B_QK
N_GATES = 16
N_GROWS = 24
KT_ROWS = A_QK_W + B_QK_W

_SPLITS = (A_QK_W, A_QK_W, V_W, V_W, N_GATES, B_QK_W, B_QK_W, V_W, V_W, D_MODEL, D_MODEL)
_OFF = [0]
for _w in _SPLITS:
    _OFF.append(_OFF[-1] + _w)
(O_QA, O_KA, O_VA, O_OA, O_G, O_QB, O_KB, O_VB, O_GB, O_MA, O_MB, _) = _OFF

PROJ_TM = 512
MERGE_TM = 256
MIX_OUT_DTYPE = jnp.bfloat16
FFN_TM = 512
FFN_TF = 256
FFN_SLOTS = 4
HALO = 8
VMEM_LIMIT = 56 * 1024 * 1024

_NT = (((1,), (1,)), ((), ()))


def _dot(a, b):
    return jnp.dot(a, b, preferred_element_type=F32)


def _dot_nt(a, b):
    return lax.dot_general(a, b, _NT, preferred_element_type=F32)


def _layer_norm(x, g, b):
    mu = jnp.mean(x, axis=-1, keepdims=True)
    xc = x - mu
    var = jnp.mean(xc * xc, axis=-1, keepdims=True)
    return xc * lax.rsqrt(var + LN_EPS) * g + b


def _resident(shape):
    nd = len(shape)
    return pl.BlockSpec(shape, lambda *_: (0,) * nd, pipeline_mode=pl.Buffered(1))


def _scan_chunks(x, op, identity, reverse):
    per_chunk = MIX_L // LANES
    blocks = [x[:, j * LANES:(j + 1) * LANES] for j in range(x.shape[1] // LANES)]
    lane = lax.broadcasted_iota(jnp.int32, blocks[0].shape, 1)
    for j, y in enumerate(blocks):
        k = 1
        while k < LANES:
            if reverse:
                y = op(y, jnp.where(lane < LANES - k, pltpu.roll(y, LANES - k, axis=1), identity))
            else:
                y = op(y, jnp.where(lane >= k, pltpu.roll(y, k, axis=1), identity))
            k *= 2
        blocks[j] = y
    for c0 in range(0, len(blocks), per_chunk):
        if reverse:
            for j in range(c0 + per_chunk - 2, c0 - 1, -1):
                blocks[j] = op(blocks[j], blocks[j + 1][:, 0:1])
        else:
            for j in range(c0 + 1, c0 + per_chunk):
                blocks[j] = op(blocks[j], blocks[j - 1][:, LANES - 1:LANES])
    return jnp.concatenate(blocks, axis=1)


def _proj_kernel(*refs, apply_ln):
    if apply_ln:
        (x_ref, lng_ref, lnb_ref, wtok_ref, wt_ref, wg_ref, bg_ref, cos_ref, sin_ref, cost_ref, sint_ref,
         qa_ref, va_ref, qb_ref, vb_ref, kt_ref, gt_ref, xn_ref) = refs
        x = _layer_norm(x_ref[...], lng_ref[...], lnb_ref[...])
        xn_ref[...] = x
    else:
        (x_ref, wtok_ref, wt_ref, wg_ref, bg_ref, cos_ref, sin_ref, cost_ref, sint_ref,
         qa_ref, va_ref, qb_ref, vb_ref, kt_ref, gt_ref) = refs
        x = x_ref[...]
    xb = x.astype(BF16)

    qa_ref[...] = (_dot(xb, wtok_ref[:, 0:A_QK_W]) * (A_QK ** -0.5)).astype(BF16)
    half = V_W // 2
    for j in range(2):
        c0 = A_QK_W + j * half
        va_ref[:, j * half:(j + 1) * half] = _dot(xb, wtok_ref[:, c0:c0 + half]).astype(BF16)
    cos = cos_ref[...]
    sin = sin_ref[...]
    hq = B_QK // 2
    for h in range(HEADS):
        c0 = A_QK_W + V_W + h * B_QK
        t = _dot(xb, wtok_ref[:, c0:c0 + B_QK])
        t1 = t[:, :hq]
        t2 = t[:, hq:]
        qb_ref[:, h * B_QK:h * B_QK + hq] = (t1 * cos - t2 * sin).astype(BF16)
        qb_ref[:, h * B_QK + hq:(h + 1) * B_QK] = (t1 * sin + t2 * cos).astype(BF16)
    for j in range(2):
        c0 = A_QK_W + V_W + B_QK_W + j * half
        vb_ref[:, j * half:(j + 1) * half] = _dot(xb, wtok_ref[:, c0:c0 + half]).astype(BF16)

    for j in range(2):
        r0 = j * 256
        kt_ref[r0:r0 + 256, :] = _dot_nt(wt_ref[r0:r0 + 256, :], xb).astype(BF16)
    cost = cost_ref[...]
    sint = sint_ref[...]
    for h in range(HEADS):
        r0 = A_QK_W + h * B_QK
        t = _dot_nt(wt_ref[r0:r0 + B_QK, :], xb)
        t1 = t[:hq, :]
        t2 = t[hq:, :]
        kt_ref[r0:r0 + hq, :] = ((t1 * cost - t2 * sint) * (B_QK ** -0.5)).astype(BF16)
        kt_ref[r0 + hq:r0 + B_QK, :] = ((t1 * sint + t2 * cost) * (B_QK ** -0.5)).astype(BF16)

    g = _dot_nt(wg_ref[...], xb) + bg_ref[...]
    row = lax.broadcasted_iota(jnp.int32, g.shape, 0)
    log_sig = jnp.minimum(g, 0.0) - jnp.log1p(jnp.exp(-jnp.abs(g)))
    gt_ref[...] = jnp.where((row & HEADS) != 0, log_sig, g)


def _gates_kernel(g_ref, o_ref):
    g = g_ref[...]
    first4 = lax.broadcasted_iota(jnp.int32, (8, g.shape[1]), 0) < HEADS
    r_pair, cb_pair, cmr_pair = [], [], []
    for d in range(2):
        gd = g[d * 8:(d + 1) * 8, :]
        cums = _scan_chunks(gd, jnp.add, 0.0, reverse=(d == 1))
        r = gd - pltpu.roll(cums, HEADS, axis=0)
        r_pair.append(r)
        cb_pair.append(cums)
        cmr_pair.append(_scan_chunks(r, jnp.maximum, -jnp.inf, reverse=(d == 1)))
    o_ref[0:8, :] = jnp.where(first4, r_pair[0], pltpu.roll(r_pair[1], HEADS, axis=0))
    o_ref[8:16, :] = jnp.where(first4, pltpu.roll(cb_pair[0], HEADS, axis=0), cb_pair[1])
    o_ref[16:24, :] = jnp.where(first4, cmr_pair[0], pltpu.roll(cmr_pair[1], HEADS, axis=0))


def _gates_call(gt):
    batch, _, seq = gt.shape
    return pl.pallas_call(
        _gates_kernel,
        grid=(batch,),
        in_specs=[pl.BlockSpec((None, N_GATES, seq), lambda b: (b, 0, 0))],
        out_specs=pl.BlockSpec((None, N_GROWS, seq), lambda b: (b, 0, 0)),
        out_shape=jax.ShapeDtypeStruct((batch, N_GROWS, seq), F32),
        compiler_params=pltpu.CompilerParams(dimension_semantics=("parallel",),
                                             vmem_limit_bytes=VMEM_LIMIT),
        name="gates",
    )(gt)


def _proj_call(x2d, batch, seq, ln, wtok, wt, wg, bg, rope):
    tokens = x2d.shape[0]
    tm = PROJ_TM
    nt = seq // tm
    cos, sin, cost, sint = rope
    tok = lambda w: pl.BlockSpec((tm, w), lambda i: (i, 0))
    in_specs = [tok(D_MODEL)]
    args = [x2d]
    if ln is not None:
        in_specs += [_resident((1, D_MODEL)), _resident((1, D_MODEL))]
        args += list(ln)
    in_specs += [_resident(wtok.shape), _resident(wt.shape), _resident(wg.shape), _resident(bg.shape),
                 pl.BlockSpec((tm, B_QK // 2), lambda i: (i % nt, 0)),
                 pl.BlockSpec((tm, B_QK // 2), lambda i: (i % nt, 0)),
                 pl.BlockSpec((B_QK // 2, tm), lambda i: (0, i % nt)),
                 pl.BlockSpec((B_QK // 2, tm), lambda i: (0, i % nt))]
    args += [wtok, wt, wg, bg, cos, sin, cost, sint]
    out_shape = [jax.ShapeDtypeStruct((tokens, A_QK_W), BF16),
                 jax.ShapeDtypeStruct((tokens, V_W), BF16),
                 jax.ShapeDtypeStruct((tokens, B_QK_W), BF16),
                 jax.ShapeDtypeStruct((tokens, V_W), BF16),
                 jax.ShapeDtypeStruct((batch, KT_ROWS, seq), BF16),
                 jax.ShapeDtypeStruct((batch, N_GATES, seq), F32)]
    out_specs = [tok(A_QK_W), tok(V_W), tok(B_QK_W), tok(V_W),
                 pl.BlockSpec((None, KT_ROWS, tm), lambda i: (i // nt, 0, i % nt)),
                 pl.BlockSpec((None, N_GATES, tm), lambda i: (i // nt, 0, i % nt))]
    if ln is not None:
        out_shape.append(jax.ShapeDtypeStruct((tokens, D_MODEL), F32))
        out_specs.append(tok(D_MODEL))
    return pl.pallas_call(
        functools.partial(_proj_kernel, apply_ln=ln is not None),
        grid=(tokens // tm,),
        in_specs=in_specs,
        out_specs=out_specs,
        out_shape=out_shape,
        compiler_params=pltpu.CompilerParams(dimension_semantics=("parallel",),
                                             vmem_limit_bytes=VMEM_LIMIT),
        name="proj",
    )(*args)


def _mixer_kernel(lg_ref,
                  qa_f, va_f, qb_f, vb_f, kt_f, g_f,
                  qa_b, va_b, qb_b, vb_b, kt_b, g_b,
                  hf_ref, hb_ref, rf_ref, rb_ref,
                  c_ref, nv_ref, m_ref, r_ref, dm_ref, wi_ref, rc_ref):
    L = MIX_L
    t_i = lax.broadcasted_iota(jnp.int32, (L, L), 0)
    s_i = lax.broadcasted_iota(jnp.int32, (L, L), 1)

    @pl.when(pl.program_id(1) == 0)
    def _init():
        c_ref[...] = jnp.zeros_like(c_ref)
        nv_ref[...] = jnp.zeros_like(nv_ref)
        m_ref[...] = jnp.zeros_like(m_ref)
        r_ref[...] = jnp.zeros_like(r_ref)
        tcol = lax.broadcasted_iota(jnp.int32, (L, LANES), 0).astype(F32)
        srow = lax.broadcasted_iota(jnp.int32, (8, L), 1).astype(F32)
        row8 = lax.broadcasted_iota(jnp.int32, (8, L), 0)
        for d in range(2):
            diff = ((t_i - s_i) if d == 0 else (s_i - t_i)).astype(F32)
            for h in range(HEADS):
                lg = lg_ref[d, h]
                dm_ref[d * HEADS + h] = jnp.where(diff >= 0.0, jnp.exp(lg * jnp.maximum(diff, 0.0)), 0.0)
                if d == 0:
                    wi = jnp.exp(lg * (tcol + 1.0))
                    ws = jnp.exp(lg * (L - 1.0 - srow))
                else:
                    wi = jnp.exp(lg * (L - tcol))
                    ws = jnp.exp(lg * srow)
                wi_ref[d * HEADS + h] = wi
                chunk_dec = jnp.exp(jnp.full((8, L), lg * float(L), F32))
                rc_ref[d * HEADS + h] = jnp.where(row8 == 0, ws, chunk_dec)

    eye = (lax.broadcasted_iota(jnp.int32, (A_QK, LANES), 0)
           == lax.broadcasted_iota(jnp.int32, (A_QK, LANES), 1))
    dirs = ((qa_f, va_f, qb_f, vb_f, kt_f, g_f, hf_ref, rf_ref),
            (qa_b, va_b, qb_b, vb_b, kt_b, g_b, hb_ref, rb_ref))
    bodies = [(d, h) for d in range(2) for h in range(HEADS)]

    def operands(d, h):
        qa, va, qb, vb, kt = dirs[d][:5]
        return dict(
            q_a=qa[:, h * A_QK:(h + 1) * A_QK],
            kt_a=kt[h * A_QK:(h + 1) * A_QK, :],
            v_a=va[:, h * HEAD_V:(h + 1) * HEAD_V],
            q_b=qb[:, h * B_QK:(h + 1) * B_QK],
            kt_b=kt[A_QK_W + h * B_QK:A_QK_W + (h + 1) * B_QK, :],
            v_b=vb[:, h * HEAD_V:(h + 1) * HEAD_V])


    prep = {}
    for d, h in bodies:
        idx = d * HEADS + h
        g = dirs[d][5]
        end = 0 if d == 1 else L - 1
        r_row = g[idx:idx + 1, :]
        cb_row = g[8 + idx:8 + idx + 1, :]
        cmr_row = g[16 + idx:16 + idx + 1, :]
        cb_col = jnp.transpose(cb_row)
        cmr_col = jnp.transpose(cmr_row)
        m_old = m_ref[idx][0:1, 0:1]
        big_m = jnp.maximum(m_old, cmr_col)
        m_top = jnp.maximum(m_old, cmr_row[:, end:end + 1])
        prep[(d, h)] = dict(
            r2_row=r_row * LOG2E, big_m2=big_m * LOG2E,
            w_inter=jnp.exp(m_old - big_m), floor=jnp.exp(-(cb_col + big_m)),
            w_s=jnp.exp(r_row - m_top), dec=jnp.exp(m_old - m_top),
            m_new=cb_row[:, end:end + 1] + m_top)

    half = L // 2
    t_h = lax.broadcasted_iota(jnp.int32, (half, half), 0)
    s_h = lax.broadcasted_iota(jnp.int32, (half, half), 1)
    st2 = {}
    for d, h in bodies:
        idx = d * HEADS + h
        o = operands(d, h)
        pr = prep[(d, h)]
        tri = (s_h >= t_h) if d == 1 else (s_h <= t_h)
        nr = slice(half, L) if d == 1 else slice(0, half)
        wr = slice(0, half) if d == 1 else slice(half, L)
        r2, bm = pr["r2_row"], pr["big_m2"]
        e_n = jnp.where(tri, jnp.exp2(r2[:, nr] - bm[nr]), 0.0)
        e_w_diag = jnp.where(tri, jnp.exp2(r2[:, wr] - bm[wr]), 0.0)
        e_w_full = jnp.exp2(r2[:, nr] - bm[wr])
        e_w = jnp.concatenate([e_w_diag, e_w_full] if d == 1 else [e_w_full, e_w_diag], axis=1)
        p_n = _dot(o["q_a"][nr], o["kt_a"][:, nr]) * e_n
        p_w = _dot(o["q_a"][wr], o["kt_a"]) * e_w
        dm = dm_ref[idx]
        a_n = _dot(o["q_b"][nr], o["kt_b"][:, nr]) * dm[nr, nr]
        a_w = _dot(o["q_b"][wr], o["kt_b"]) * dm[wr, :]
        kw_t = o["kt_a"].astype(F32) * pr["w_s"]
        kbw_t = o["kt_b"].astype(F32) * rc_ref[idx][0:1, :]
        st2[(d, h)] = dict(
            blocks=((nr, nr, p_n.astype(BF16), jnp.sum(p_n, axis=1, keepdims=True), a_n.astype(BF16)),
                    (wr, slice(0, L), p_w.astype(BF16), jnp.sum(p_w, axis=1, keepdims=True), a_w.astype(BF16))),
            kw=kw_t.astype(BF16), kwsum=jnp.sum(kw_t, axis=1, keepdims=True), kbw=kbw_t.astype(BF16))

    for d, h in bodies:
        idx = d * HEADS + h
        o = operands(d, h)
        pr, s2 = prep[(d, h)], st2[(d, h)]
        h_out, r_out = dirs[d][6], dirs[d][7]
        c_old = c_ref[idx]
        r_old = r_ref[idx]
        nv_old = nv_ref[idx]
        wi = wi_ref[idx]
        wi2 = jnp.concatenate([wi, wi], axis=1)
        nv_row = jnp.sum(jnp.where(eye, nv_old, 0.0), axis=0, keepdims=True)
        q_nv = jnp.sum(o["q_a"].astype(F32) * nv_row, axis=1, keepdims=True)
        qc = _dot(o["q_a"], c_old.astype(BF16))
        qr = _dot(o["q_b"], r_old.astype(BF16))
        for rows, cols, pb, psum, ab in s2["blocks"]:
            den = psum + pr["w_inter"][rows] * q_nv[rows]
            inv = 1.0 / jnp.maximum(jnp.abs(den), pr["floor"][rows])
            num = _dot(pb, o["v_a"][cols]) + pr["w_inter"][rows] * qc[rows]
            h_out[rows, h * HEAD_V:(h + 1) * HEAD_V] = (num * inv).astype(h_out.dtype)
            r_out[rows, h * HEAD_V:(h + 1) * HEAD_V] = (
                _dot(ab, o["v_b"][cols]) + wi2[rows] * qr[rows]).astype(r_out.dtype)
        c_ref[idx] = pr["dec"] * c_old + _dot(s2["kw"], o["v_a"])
        nv_ref[idx] = pr["dec"] * nv_old + s2["kwsum"]
        m_ref[idx] = jnp.broadcast_to(pr["m_new"], m_ref.shape[1:])
        r_ref[idx] = rc_ref[idx][1:2, 0:1] * r_old + _dot(s2["kbw"], o["v_b"])


def _mixer_call(lg, qa, va, qb, vb, kt, gt, batch, seq):
    L = MIX_L
    n = seq // L
    qa, va, qb, vb = (a.reshape(batch, seq, a.shape[-1]) for a in (qa, va, qb, vb))

    def specs(chunk_of):
        tokm = lambda w: pl.BlockSpec((None, L, w), lambda b, c: (b, chunk_of(c), 0))
        featm = lambda r: pl.BlockSpec((None, r, L), lambda b, c: (b, 0, chunk_of(c)))
        return [tokm(A_QK_W), tokm(V_W), tokm(B_QK_W), tokm(V_W), featm(KT_ROWS), featm(N_GROWS)]

    fwd = lambda c: c
    bwd = lambda c: n - 1 - c
    out_f = pl.BlockSpec((None, L, V_W), lambda b, c: (b, c, 0))
    out_b = pl.BlockSpec((None, L, V_W), lambda b, c: (b, n - 1 - c, 0))
    out_sds = jax.ShapeDtypeStruct((batch, seq, V_W), MIX_OUT_DTYPE)
    nstate = 2 * HEADS
    return pl.pallas_call(
        _mixer_kernel,
        grid=(batch, n),
        in_specs=[pl.BlockSpec(memory_space=pltpu.SMEM)] + specs(fwd) + specs(bwd),
        out_specs=[out_f, out_b, out_f, out_b],
        out_shape=[out_sds] * 4,
        scratch_shapes=[pltpu.VMEM((nstate, A_QK, HEAD_V), F32),
                        pltpu.VMEM((nstate, A_QK, LANES), F32),
                        pltpu.VMEM((nstate, 8, LANES), F32),
                        pltpu.VMEM((nstate, B_QK, HEAD_V), F32),
                        pltpu.VMEM((nstate, L, L), F32),
                        pltpu.VMEM((nstate, L, LANES), F32),
                        pltpu.VMEM((nstate, 8, L), F32)],
        compiler_params=pltpu.CompilerParams(dimension_semantics=("parallel", "arbitrary"),
                                             vmem_limit_bytes=VMEM_LIMIT),
        name="mixer",
    )(lg, qa, va, qb, vb, kt, gt, qa, va, qb, vb, kt, gt)


def _head_norm(h):
    mu = jnp.mean(h, axis=-1, keepdims=True)
    hc = h - mu
    var = jnp.mean(hc * hc, axis=-1, keepdims=True)
    return hc * lax.rsqrt(var + LN_EPS)


def _merge_kernel(x_ref, hf_ref, hb_ref, rf_ref, rb_ref, w4_ref, ga_ref, gb_ref, wo_ref, lng_ref, lnb_ref,
                  o_ref, merged_ref, *, alpha):
    x = x_ref[...]
    xb = x.astype(BF16)
    for h in range(HEADS):
        sl = slice(h * HEAD_V, (h + 1) * HEAD_V)
        col = lambda piece: slice(piece * V_W + h * HEAD_V, piece * V_W + (h + 1) * HEAD_V)
        o_gate = jax.nn.sigmoid(_dot(xb, w4_ref[:, col(0)]))
        both = lambda f_ref, b_ref: f_ref[:, sl].astype(F32) + b_ref[:, sl].astype(F32)
        y_a = _head_norm(both(hf_ref, hb_ref) * o_gate) * ga_ref[:, sl]
        swish = jax.nn.silu(_dot(xb, w4_ref[:, col(1)]))
        y_b = _head_norm(both(rf_ref, rb_ref)) * gb_ref[:, sl] * swish
        m_a = jax.nn.sigmoid(_dot(xb, w4_ref[:, col(2)]))
        m_b = jax.nn.sigmoid(_dot(xb, w4_ref[:, col(3)]))
        merged_ref[:, sl] = (m_a * y_a + m_b * y_b).astype(BF16)
    z = alpha * x + _dot(merged_ref[...], wo_ref[...])
    o_ref[...] = _layer_norm(z, lng_ref[...], lnb_ref[...])


def _merge_call(x2d, hf, hb, rf, rb, w4, g_a, g_b, wo, ln_g, ln_b, alpha):
    tokens = x2d.shape[0]
    tm = MERGE_TM
    tok = pl.BlockSpec((tm, D_MODEL), lambda i: (i, 0))
    flat = lambda a: a.reshape(tokens, V_W)
    return pl.pallas_call(
        functools.partial(_merge_kernel, alpha=alpha),
        grid=(tokens // tm,),
        in_specs=[tok, tok, tok, tok, tok, _resident(w4.shape), _resident(g_a.shape), _resident(g_b.shape),
                  _resident(wo.shape), _resident(ln_g.shape), _resident(ln_b.shape)],
        out_specs=tok,
        out_shape=jax.ShapeDtypeStruct((tokens, D_MODEL), F32),
        scratch_shapes=[pltpu.VMEM((tm, V_W), BF16)],
        compiler_params=pltpu.CompilerParams(dimension_semantics=("parallel",),
                                             vmem_limit_bytes=VMEM_LIMIT),
        name="merge",
    )(x2d, flat(hf), flat(hb), flat(rf), flat(rb), w4, g_a, g_b, wo, ln_g, ln_b)


def _ffn_kernel(x_ref, xp_ref, xn_ref, wup_ref, cw_ref, cb_ref, wdn_ref, lng_ref, lnb_ref, o_ref,
                acc_ref, hv_ref, hg_ref, *, alpha, tiles_per_seq):
    tm = FFN_TM
    rows = tm + 2 * HALO
    n_slabs = FFN_TF // LANES
    pos = lax.rem(pl.program_id(0), tiles_per_seq)
    x = x_ref[...]
    x_prev = jnp.where(pos == 0, 0.0, xp_ref[...])
    x_next = jnp.where(pos == tiles_per_seq - 1, 0.0, xn_ref[...])
    xe = jnp.concatenate([x_prev, x, x_next], axis=0).astype(BF16)

    def put(h_ref, slot, hu):
        for c in range(n_slabs):
            h_ref[slot, c, pl.ds(0, rows, stride=2), :] = hu[:, c * LANES:(c + 1) * LANES]

    def up(j):
        c_val = j * FFN_TF
        c_gate = D_FF + j * FFN_TF
        put(hv_ref, j % FFN_SLOTS, _dot(xe, wup_ref[:, c_val:c_val + FFN_TF]))
        put(hg_ref, j % FFN_SLOTS, _dot(xe, wup_ref[:, c_gate:c_gate + FFN_TF]))

    def conv(h_ref, slot, c0):
        out = []
        for c in range(n_slabs):
            cols = slice(c0 + c * LANES, c0 + (c + 1) * LANES)
            w = cw_ref[:, cols]
            taps = [h_ref[slot, c, pl.ds(2 * (HALO - 1 + k), tm, stride=2), :] for k in range(3)]
            out.append(w[0:1] * taps[0] + w[1:2] * taps[1] + w[2:3] * taps[2] + cb_ref[:, cols])
        return jnp.concatenate(out, axis=1)

    n_chunks = D_FF // FFN_TF
    for j in range(FFN_SLOTS - 1):
        up(j)
    for j in range(n_chunks):
        if j + FFN_SLOTS - 1 < n_chunks:
            up(j + FFN_SLOTS - 1)
        c_val = j * FFN_TF
        slot = j % FFN_SLOTS
        f = (jax.nn.gelu(conv(hg_ref, slot, D_FF + c_val)) * conv(hv_ref, slot, c_val)).astype(BF16)
        part = _dot(f, wdn_ref[c_val:c_val + FFN_TF, :])
        if j == 0:
            acc_ref[...] = part
        else:
            acc_ref[...] += part
    o_ref[...] = _layer_norm(alpha * x + acc_ref[...], lng_ref[...], lnb_ref[...])


def _ffn_call(x2d, seq, wup, cw, cb, wdn, ln_g, ln_b, alpha):
    tokens = x2d.shape[0]
    tm = FFN_TM
    per = tm // HALO
    last = tokens // HALO - 1
    tok = pl.BlockSpec((tm, D_MODEL), lambda i: (i, 0))
    return pl.pallas_call(
        functools.partial(_ffn_kernel, alpha=alpha, tiles_per_seq=seq // tm),
        grid=(tokens // tm,),
        in_specs=[tok,
                  pl.BlockSpec((HALO, D_MODEL), lambda i: (jnp.maximum(i * per - 1, 0), 0)),
                  pl.BlockSpec((HALO, D_MODEL), lambda i: (jnp.minimum((i + 1) * per, last), 0)),
                  _resident(wup.shape), _resident(cw.shape), _resident(cb.shape), _resident(wdn.shape),
                  _resident(ln_g.shape), _resident(ln_b.shape)],
        out_specs=tok,
        out_shape=jax.ShapeDtypeStruct((tokens, D_MODEL), F32),
        scratch_shapes=[pltpu.VMEM((tm, D_MODEL), F32),
                        pltpu.VMEM((FFN_SLOTS, FFN_TF // LANES, 2 * (tm + 2 * HALO), LANES), F32),
                        pltpu.VMEM((FFN_SLOTS, FFN_TF // LANES, 2 * (tm + 2 * HALO), LANES), F32)],
        compiler_params=pltpu.CompilerParams(dimension_semantics=("parallel",),
                                             vmem_limit_bytes=VMEM_LIMIT),
        name="ffn",
    )(x2d, x2d, x2d, wup, cw, cb, wdn, ln_g, ln_b)


def _rope_tables(seq):
    inv_freq = ROPE_BASE ** (-jnp.arange(0, B_QK, 2, dtype=F32) / B_QK)
    ang = jnp.arange(seq, dtype=F32)[:, None] * inv_freq[None, :]
    cos = jnp.cos(ang)
    sin = jnp.sin(ang)
    return cos, sin, cos.T, sin.T


def _layer_weights(l, w_in, b_if, w_o, w_up, w_down):
    w = w_in[l]
    cols = lambda o, n: w[:, o:o + n]
    wtok = jnp.concatenate([cols(O_QA, A_QK_W), cols(O_VA, V_W), cols(O_QB, B_QK_W), cols(O_VB, V_W)],
                           axis=1).astype(BF16)
    wt = jnp.concatenate([cols(O_KA, A_QK_W), cols(O_KB, B_QK_W)], axis=1).T.astype(BF16)
    wg = cols(O_G, N_GATES).T.astype(BF16)
    bg = b_if[l].astype(F32).reshape(N_GATES, 1)
    w4 = jnp.concatenate([cols(O_OA, V_W), cols(O_GB, V_W), cols(O_MA, D_MODEL), cols(O_MB, D_MODEL)],
                         axis=1).astype(BF16)
    return wtok, wt, wg, bg, w4, w_o[l].astype(BF16), w_up[l].astype(BF16), w_down[l].astype(BF16)


def _trunk(x, ln_in, layers, alpha):
    batch, seq, _ = x.shape
    assert seq % PROJ_TM == 0 and seq % FFN_TM == 0 and seq % MIX_L == 0
    rope = _rope_tables(seq)
    x2d = x.reshape(batch * seq, D_MODEL)
    for l, lw in enumerate(layers):
        (wtok, wt, wg, bg, w4, wo, wup, wdn, lg, g_a, g_b, ln1, cw, cb, ln2) = lw
        outs = _proj_call(x2d, batch, seq, ln_in if l == 0 else None, wtok, wt, wg, bg, rope)
        qa, va, qb, vb, kt, gt = outs[:6]
        if l == 0:
            x2d = outs[6]
        hf, hb, rf, rb = _mixer_call(lg, qa, va, qb, vb, kt, _gates_call(gt), batch, seq)
        x2d = _merge_call(x2d, hf, hb, rf, rb, w4, g_a, g_b, wo, ln1[0], ln1[1], alpha)
        x2d = _ffn_call(x2d, seq, wup, cw, cb, wdn, ln2[0], ln2[1], alpha)
    return x2d.reshape(batch, seq, D_MODEL)


def kernel(x_prompt, x_sample, ln_in_g, ln_in_b, w_in, b_if, ret_log_decay, mlstm_norm_g, ret_norm_g, w_o, ln1_g, ln1_b, w_up, conv_w, conv_b, w_down, ln2_g, ln2_b):
    depth = w_in.shape[0]
    alpha = (2 * depth) ** 0.25
    row = lambda a: a.astype(F32).reshape(1, -1)
    layers = []
    for l in range(depth):
        layers.append(_layer_weights(l, w_in, b_if, w_o, w_up, w_down) + (
            ret_log_decay[l].astype(F32), row(mlstm_norm_g[l]), row(ret_norm_g[l]),
            (row(ln1_g[l]), row(ln1_b[l])), conv_w[l].astype(F32), row(conv_b[l]),
            (row(ln2_g[l]), row(ln2_b[l]))))
    ln_in = (row(ln_in_g), row(ln_in_b))
    return (_trunk(x_prompt, ln_in, layers, alpha), _trunk(x_sample, ln_in, layers, alpha))
```

```python
import functools

import jax
import jax.numpy as jnp
from jax import lax
from jax.experimental import pallas as pl
from jax.experimental.pallas import tpu as pltpu

F32 = jnp.float32
BF16 = jnp.bfloat16

D_MODEL = 1024
HEADS = 4
A_QK = 128
HEAD_V = 256
B_QK = 256
D_FF = 2816
MIX_L = 256
MIX_SUB = 2
LANES = 128
LOG2E = 1.4426950408889634
LN_EPS = 1e-5
ROPE_BASE = 10000.0

A_QK_W = HEADS * A_QK
V_W = HEADS * HEAD_V
B_QK_W = HEADS * B_QK
N_GATES = 16
N_GROWS = 24
KT_ROWS = A_QK_W + B_QK_W

_SPLITS = (A_QK_W, A_QK_W, V_W, V_W, N_GATES, B_QK_W, B_QK_W, V_W, V_W, D_MODEL, D_MODEL)
_OFF = [0]
for _w in _SPLITS:
    _OFF.append(_OFF[-1] + _w)
(O_QA, O_KA, O_VA, O_OA, O_G, O_QB, O_KB, O_VB, O_GB, O_MA, O_MB, _) = _OFF

PROJ_TM = 512
MERGE_TM = 256
MIX_OUT_DTYPE = jnp.bfloat16
FFN_TM = 512
FFN_TF = 256
FFN_SLOTS = 4
HALO = 8
VMEM_LIMIT = 56 * 1024 * 1024

_NT = (((1,), (1,)), ((), ()))


def _dot(a, b):
    return jnp.dot(a, b, preferred_element_type=F32)


def _dot_nt(a, b):
    return lax.dot_general(a, b, _NT, preferred_element_type=F32)


def _layer_norm(x, g, b):
    mu = jnp.mean(x, axis=-1, keepdims=True)
    xc = x - mu
    var = jnp.mean(xc * xc, axis=-1, keepdims=True)
    return xc * lax.rsqrt(var + LN_EPS) * g + b


def _resident(shape):
    nd = len(shape)
    return pl.BlockSpec(shape, lambda *_: (0,) * nd, pipeline_mode=pl.Buffered(1))


def _scan_chunks(x, op, identity, reverse):
    per_chunk = MIX_L // LANES
    blocks = [x[:, j * LANES:(j + 1) * LANES] for j in range(x.shape[1] // LANES)]
    lane = lax.broadcasted_iota(jnp.int32, blocks[0].shape, 1)
    for j, y in enumerate(blocks):
        k = 1
        while k < LANES:
            if reverse:
                y = op(y, jnp.where(lane < LANES - k, pltpu.roll(y, LANES - k, axis=1), identity))
            else:
                y = op(y, jnp.where(lane >= k, pltpu.roll(y, k, axis=1), identity))
            k *= 2
        blocks[j] = y
    for c0 in range(0, len(blocks), per_chunk):
        if reverse:
            for j in range(c0 + per_chunk - 2, c0 - 1, -1):
                blocks[j] = op(blocks[j], blocks[j + 1][:, 0:1])
        else:
            for j in range(c0 + 1, c0 + per_chunk):
                blocks[j] = op(blocks[j], blocks[j - 1][:, LANES - 1:LANES])
    return jnp.concatenate(blocks, axis=1)


def _proj_kernel(*refs, apply_ln):
    if apply_ln:
        (x_ref, lng_ref, lnb_ref, wtok_ref, wt_ref, wg_ref, bg_ref, cos_ref, sin_ref, cost_ref, sint_ref,
         qa_ref, va_ref, qb_ref, vb_ref, kt_ref, gt_ref, xn_ref) = refs
        x = _layer_norm(x_ref[...], lng_ref[...], lnb_ref[...])
        xn_ref[...] = x
    else:
        (x_ref, wtok_ref, wt_ref, wg_ref, bg_ref, cos_ref, sin_ref, cost_ref, sint_ref,
         qa_ref, va_ref, qb_ref, vb_ref, kt_ref, gt_ref) = refs
        x = x_ref[...]
    xb = x.astype(BF16)

    qa_ref[...] = (_dot(xb, wtok_ref[:, 0:A_QK_W]) * (A_QK ** -0.5)).astype(BF16)
    half = V_W // 2
    for j in range(2):
        c0 = A_QK_W + j * half
        va_ref[:, j * half:(j + 1) * half] = _dot(xb, wtok_ref[:, c0:c0 + half]).astype(BF16)
    cos = cos_ref[...]
    sin = sin_ref[...]
    hq = B_QK // 2
    for h in range(HEADS):
        c0 = A_QK_W + V_W + h * B_QK
        t = _dot(xb, wtok_ref[:, c0:c0 + B_QK])
        t1 = t[:, :hq]
        t2 = t[:, hq:]
        qb_ref[:, h * B_QK:h * B_QK + hq] = (t1 * cos - t2 * sin).astype(BF16)
        qb_ref[:, h * B_QK + hq:(h + 1) * B_QK] = (t1 * sin + t2 * cos).astype(BF16)
    for j in range(2):
        c0 = A_QK_W + V_W + B_QK_W + j * half
        vb_ref[:, j * half:(j + 1) * half] = _dot(xb, wtok_ref[:, c0:c0 + half]).astype(BF16)

    for j in range(2):
        r0 = j * 256
        kt_ref[r0:r0 + 256, :] = _dot_nt(wt_ref[r0:r0 + 256, :], xb).astype(BF16)
    cost = cost_ref[...]
    sint = sint_ref[...]
    for h in range(HEADS):
        r0 = A_QK_W + h * B_QK
        t = _dot_nt(wt_ref[r0:r0 + B_QK, :], xb)
        t1 = t[:hq, :]
        t2 = t[hq:, :]
        kt_ref[r0:r0 + hq, :] = ((t1 * cost - t2 * sint) * (B_QK ** -0.5)).astype(BF16)
        kt_ref[r0 + hq:r0 + B_QK, :] = ((t1 * sint + t2 * cost) * (B_QK ** -0.5)).astype(BF16)

    g = _dot_nt(wg_ref[...], xb) + bg_ref[...]
    row = lax.broadcasted_iota(jnp.int32, g.shape, 0)
    log_sig = jnp.minimum(g, 0.0) - jnp.log1p(jnp.exp(-jnp.abs(g)))
    gt_ref[...] = jnp.where((row & HEADS) != 0, log_sig, g)


def _gates_kernel(g_ref, o_ref):
    g = g_ref[...]
    first4 = lax.broadcasted_iota(jnp.int32, (8, g.shape[1]), 0) < HEADS
    r_pair, cb_pair, cmr_pair = [], [], []
    for d in range(2):
        gd = g[d * 8:(d + 1) * 8, :]
        cums = _scan_chunks(gd, jnp.add, 0.0, reverse=(d == 1))
        r = gd - pltpu.roll(cums, HEADS, axis=0)
        r_pair.append(r)
        cb_pair.append(cums)
        cmr_pair.append(_scan_chunks(r, jnp.maximum, -jnp.inf, reverse=(d == 1)))
    o_ref[0:8, :] = jnp.where(first4, r_pair[0], pltpu.roll(r_pair[1], HEADS, axis=0))
    o_ref[8:16, :] = jnp.where(first4, pltpu.roll(cb_pair[0], HEADS, axis=0), cb_pair[1])
    o_ref[16:24, :] = jnp.where(first4, cmr_pair[0], pltpu.roll(cmr_pair[1], HEADS, axis=0))


def _gates_call(gt):
    batch, _, seq = gt.shape
    return pl.pallas_call(
        _gates_kernel,
        grid=(batch,),
        in_specs=[pl.BlockSpec((None, N_GATES, seq), lambda b: (b, 0, 0))],
        out_specs=pl.BlockSpec((None, N_GROWS, seq), lambda b: (b, 0, 0)),
        out_shape=jax.ShapeDtypeStruct((batch, N_GROWS, seq), F32),
        compiler_params=pltpu.CompilerParams(dimension_semantics=("parallel",),
                                             vmem_limit_bytes=VMEM_LIMIT),
        name="gates",
    )(gt)


def _proj_call(x2d, batch, seq, ln, wtok, wt, wg, bg, rope):
    tokens = x2d.shape[0]
    tm = PROJ_TM
    nt = seq // tm
    cos, sin, cost, sint = rope
    tok = lambda w: pl.BlockSpec((tm, w), lambda i: (i, 0))
    in_specs = [tok(D_MODEL)]
    args = [x2d]
    if ln is not None:
        in_specs += [_resident((1, D_MODEL)), _resident((1, D_MODEL))]
        args += list(ln)
    in_specs += [_resident(wtok.shape), _resident(wt.shape), _resident(wg.shape), _resident(bg.shape),
                 pl.BlockSpec((tm, B_QK // 2), lambda i: (i % nt, 0)),
                 pl.BlockSpec((tm, B_QK // 2), lambda i: (i % nt, 0)),
                 pl.BlockSpec((B_QK // 2, tm), lambda i: (0, i % nt)),
                 pl.BlockSpec((B_QK // 2, tm), lambda i: (0, i % nt))]
    args += [wtok, wt, wg, bg, cos, sin, cost, sint]
    out_shape = [jax.ShapeDtypeStruct((tokens, A_QK_W), BF16),
                 jax.ShapeDtypeStruct((tokens, V_W), BF16),
                 jax.ShapeDtypeStruct((tokens, B_QK_W), BF16),
                 jax.ShapeDtypeStruct((tokens, V_W), BF16),
                 jax.ShapeDtypeStruct((batch, KT_ROWS, seq), BF16),
                 jax.ShapeDtypeStruct((batch, N_GATES, seq), F32)]
    out_specs = [tok(A_QK_W), tok(V_W), tok(B_QK_W), tok(V_W),
                 pl.BlockSpec((None, KT_ROWS, tm), lambda i: (i // nt, 0, i % nt)),
                 pl.BlockSpec((None, N_GATES, tm), lambda i: (i // nt, 0, i % nt))]
    if ln is not None:
        out_shape.append(jax.ShapeDtypeStruct((tokens, D_MODEL), F32))
        out_specs.append(tok(D_MODEL))
    return pl.pallas_call(
        functools.partial(_proj_kernel, apply_ln=ln is not None),
        grid=(tokens // tm,),
        in_specs=in_specs,
        out_specs=out_specs,
        out_shape=out_shape,
        compiler_params=pltpu.CompilerParams(dimension_semantics=("parallel",),
                                             vmem_limit_bytes=VMEM_LIMIT),
        name="proj",
    )(*args)


def _mixer_kernel(lg_ref,
                  qa_f, va_f, qb_f, vb_f, kt_f, g_f,
                  qa_b, va_b, qb_b, vb_b, kt_b, g_b,
                  hf_ref, hb_ref, rf_ref, rb_ref,
                  c_ref, nv_ref, m_ref, r_ref, dm_ref, wi_ref, rc_ref):
    L = MIX_L
    t_i = lax.broadcasted_iota(jnp.int32, (L, L), 0)
    s_i = lax.broadcasted_iota(jnp.int32, (L, L), 1)

    @pl.when(pl.program_id(1) == 0)
    def _init():
        c_ref[...] = jnp.zeros_like(c_ref)
        nv_ref[...] = jnp.zeros_like(nv_ref)
        m_ref[...] = jnp.zeros_like(m_ref)
        r_ref[...] = jnp.zeros_like(r_ref)
        tcol = lax.broadcasted_iota(jnp.int32, (L, LANES), 0).astype(F32)
        srow = lax.broadcasted_iota(jnp.int32, (8, L), 1).astype(F32)
        row8 = lax.broadcasted_iota(jnp.int32, (8, L), 0)
        for d in range(2):
            diff = ((t_i - s_i) if d == 0 else (s_i - t_i)).astype(F32)
            for h in range(HEADS):
                lg = lg_ref[d, h]
                dm_ref[d * HEADS + h] = jnp.where(diff >= 0.0, jnp.exp(lg * jnp.maximum(diff, 0.0)), 0.0)
                if d == 0:
                    wi = jnp.exp(lg * (tcol + 1.0))
                    ws = jnp.exp(lg * (L - 1.0 - srow))
                else:
                    wi = jnp.exp(lg * (L - tcol))
                    ws = jnp.exp(lg * srow)
                wi_ref[d * HEADS + h] = wi
                chunk_dec = jnp.exp(jnp.full((8, L), lg * float(L), F32))
                rc_ref[d * HEADS + h] = jnp.where(row8 == 0, ws, chunk_dec)

    for sub in range(MIX_SUB):
        views = []
        for off, refs in ((sub * L, (qa_f, va_f, qb_f, vb_f, kt_f, g_f, hf_ref, rf_ref)),
                          ((MIX_SUB - 1 - sub) * L, (qa_b, va_b, qb_b, vb_b, kt_b, g_b, hb_ref, rb_ref))):
            qa, va, qb, vb, kt, g, h_out, r_out = refs
            tok = lambda ref: ref.at[off:off + L, :]
            views.append((tok(qa), tok(va), tok(qb), tok(vb), kt.at[:, off:off + L], g.at[:, off:off + L],
                          tok(h_out), tok(r_out)))
        _mixer_chunk(tuple(views), c_ref, nv_ref, m_ref, r_ref, dm_ref, wi_ref, rc_ref)


def _mixer_chunk(dirs, c_ref, nv_ref, m_ref, r_ref, dm_ref, wi_ref, rc_ref):
    L = MIX_L
    eye = (lax.broadcasted_iota(jnp.int32, (A_QK, LANES), 0)
           == lax.broadcasted_iota(jnp.int32, (A_QK, LANES), 1))
    bodies = [(d, h) for d in range(2) for h in range(HEADS)]

    def operands(d, h):
        qa, va, qb, vb, kt = dirs[d][:5]
        return dict(
            q_a=qa[:, h * A_QK:(h + 1) * A_QK],
            kt_a=kt[h * A_QK:(h + 1) * A_QK, :],
            v_a=va[:, h * HEAD_V:(h + 1) * HEAD_V],
            q_b=qb[:, h * B_QK:(h + 1) * B_QK],
            kt_b=kt[A_QK_W + h * B_QK:A_QK_W + (h + 1) * B_QK, :],
            v_b=vb[:, h * HEAD_V:(h + 1) * HEAD_V])


    prep = {}
    for d, h in bodies:
        idx = d * HEADS + h
        g = dirs[d][5]
        end = 0 if d == 1 else L - 1
        r_row = g[idx:idx + 1, :]
        cb_row = g[8 + idx:8 + idx + 1, :]
        cmr_row = g[16 + idx:16 + idx + 1, :]
        cb_col = jnp.transpose(cb_row)
        cmr_col = jnp.transpose(cmr_row)
        m_old = m_ref[idx][0:1, 0:1]
        big_m = jnp.maximum(m_old, cmr_col)
        m_top = jnp.maximum(m_old, cmr_row[:, end:end + 1])
        prep[(d, h)] = dict(
            r2_row=r_row * LOG2E, big_m2=big_m * LOG2E,
            w_inter=jnp.exp(m_old - big_m), floor=jnp.exp(-(cb_col + big_m)),
            w_s=jnp.exp(r_row - m_top), dec=jnp.exp(m_old - m_top),
            m_new=cb_row[:, end:end + 1] + m_top)

    half = L // 2
    t_h = lax.broadcasted_iota(jnp.int32, (half, half), 0)
    s_h = lax.broadcasted_iota(jnp.int32, (half, half), 1)
    st2 = {}
    for d, h in bodies:
        idx = d * HEADS + h
        o = operands(d, h)
        pr = prep[(d, h)]
        tri = (s_h >= t_h) if d == 1 else (s_h <= t_h)
        nr = slice(half, L) if d == 1 else slice(0, half)
        wr = slice(0, half) if d == 1 else slice(half, L)
        r2, bm = pr["r2_row"], pr["big_m2"]
        e_n = jnp.where(tri, jnp.exp2(r2[:, nr] - bm[nr]), 0.0)
        e_w_diag = jnp.where(tri, jnp.exp2(r2[:, wr] - bm[wr]), 0.0)
        e_w_full = jnp.exp2(r2[:, nr] - bm[wr])
        e_w = jnp.concatenate([e_w_diag, e_w_full] if d == 1 else [e_w_full, e_w_diag], axis=1)
        p_n = _dot(o["q_a"][nr], o["kt_a"][:, nr]) * e_n
        p_w = _dot(o["q_a"][wr], o["kt_a"]) * e_w
        dm = dm_ref[idx]
        a_n = _dot(o["q_b"][nr], o["kt_b"][:, nr]) * dm[nr, nr]
        a_w = _dot(o["q_b"][wr], o["kt_b"]) * dm[wr, :]
        kw_t = o["kt_a"].astype(F32) * pr["w_s"]
        kbw_t = o["kt_b"].astype(F32) * rc_ref[idx][0:1, :]
        st2[(d, h)] = dict(
            blocks=((nr, nr, p_n.astype(BF16), jnp.sum(p_n, axis=1, keepdims=True), a_n.astype(BF16)),
                    (wr, slice(0, L), p_w.astype(BF16), jnp.sum(p_w, axis=1, keepdims=True), a_w.astype(BF16))),
            kw=kw_t.astype(BF16), kwsum=jnp.sum(kw_t, axis=1, keepdims=True), kbw=kbw_t.astype(BF16))

    for d, h in bodies:
        idx = d * HEADS + h
        o = operands(d, h)
        pr, s2 = prep[(d, h)], st2[(d, h)]
        h_out, r_out = dirs[d][6], dirs[d][7]
        c_old = c_ref[idx]
        r_old = r_ref[idx]
        nv_old = nv_ref[idx]
        wi = wi_ref[idx]
        wi2 = jnp.concatenate([wi, wi], axis=1)
        nv_row = jnp.sum(jnp.where(eye, nv_old, 0.0), axis=0, keepdims=True)
        q_nv = jnp.sum(o["q_a"].astype(F32) * nv_row, axis=1, keepdims=True)
        qc = _dot(o["q_a"], c_old.astype(BF16))
        qr = _dot(o["q_b"], r_old.astype(BF16))
        for rows, cols, pb, psum, ab in s2["blocks"]:
            den = psum + pr["w_inter"][rows] * q_nv[rows]
            inv = 1.0 / jnp.maximum(jnp.abs(den), pr["floor"][rows])
            num = _dot(pb, o["v_a"][cols]) + pr["w_inter"][rows] * qc[rows]
            h_out[rows, h * HEAD_V:(h + 1) * HEAD_V] = (num * inv).astype(h_out.dtype)
            r_out[rows, h * HEAD_V:(h + 1) * HEAD_V] = (
                _dot(ab, o["v_b"][cols]) + wi2[rows] * qr[rows]).astype(r_out.dtype)
        c_ref[idx] = pr["dec"] * c_old + _dot(s2["kw"], o["v_a"])
        nv_ref[idx] = pr["dec"] * nv_old + s2["kwsum"]
        m_ref[idx] = jnp.broadcast_to(pr["m_new"], m_ref.shape[1:])
        r_ref[idx] = rc_ref[idx][1:2, 0:1] * r_old + _dot(s2["kbw"], o["v_b"])


def _mixer_call(lg, qa, va, qb, vb, kt, gt, batch, seq):
    L = MIX_L
    step = MIX_SUB * L
    n = seq // step
    qa, va, qb, vb = (a.reshape(batch, seq, a.shape[-1]) for a in (qa, va, qb, vb))

    def specs(block_of):
        tokm = lambda w: pl.BlockSpec((None, step, w), lambda b, c: (b, block_of(c), 0))
        featm = lambda r: pl.BlockSpec((None, r, step), lambda b, c: (b, 0, block_of(c)))
        return [tokm(A_QK_W), tokm(V_W), tokm(B_QK_W), tokm(V_W), featm(KT_ROWS), featm(N_GROWS)]

    fwd = lambda c: c
    bwd = lambda c: n - 1 - c
    out_f = pl.BlockSpec((None, step, V_W), lambda b, c: (b, c, 0))
    out_b = pl.BlockSpec((None, step, V_W), lambda b, c: (b, n - 1 - c, 0))
    out_sds = jax.ShapeDtypeStruct((batch, seq, V_W), MIX_OUT_DTYPE)
    nstate = 2 * HEADS
    return pl.pallas_call(
        _mixer_kernel,
        grid=(batch, n),
        in_specs=[pl.BlockSpec(memory_space=pltpu.SMEM)] + specs(fwd) + specs(bwd),
        out_specs=[out_f, out_b, out_f, out_b],
        out_shape=[out_sds] * 4,
        scratch_shapes=[pltpu.VMEM((nstate, A_QK, HEAD_V), F32),
                        pltpu.VMEM((nstate, A_QK, LANES), F32),
                        pltpu.VMEM((nstate, 8, LANES), F32),
                        pltpu.VMEM((nstate, B_QK, HEAD_V), F32),
                        pltpu.VMEM((nstate, L, L), F32),
                        pltpu.VMEM((nstate, L, LANES), F32),
                        pltpu.VMEM((nstate, 8, L), F32)],
        compiler_params=pltpu.CompilerParams(dimension_semantics=("parallel", "arbitrary"),
                                             vmem_limit_bytes=VMEM_LIMIT),
        name="mixer",
    )(lg, qa, va, qb, vb, kt, gt, qa, va, qb, vb, kt, gt)


def _head_norm(h):
    mu = jnp.mean(h, axis=-1, keepdims=True)
    hc = h - mu
    var = jnp.mean(hc * hc, axis=-1, keepdims=True)
    return hc * lax.rsqrt(var + LN_EPS)


def _merge_kernel(x_ref, hf_ref, hb_ref, rf_ref, rb_ref, w4_ref, ga_ref, gb_ref, wo_ref, lng_ref, lnb_ref,
                  o_ref, merged_ref, *, alpha):
    x = x_ref[...]
    xb = x.astype(BF16)
    for h in range(HEADS):
        sl = slice(h * HEAD_V, (h + 1) * HEAD_V)
        col = lambda piece: slice(piece * V_W + h * HEAD_V, piece * V_W + (h + 1) * HEAD_V)
        o_gate = jax.nn.sigmoid(_dot(xb, w4_ref[:, col(0)]))
        both = lambda f_ref, b_ref: f_ref[:, sl].astype(F32) + b_ref[:, sl].astype(F32)
        y_a = _head_norm(both(hf_ref, hb_ref) * o_gate) * ga_ref[:, sl]
        swish = jax.nn.silu(_dot(xb, w4_ref[:, col(1)]))
        y_b = _head_norm(both(rf_ref, rb_ref)) * gb_ref[:, sl] * swish
        m_a = jax.nn.sigmoid(_dot(xb, w4_ref[:, col(2)]))
        m_b = jax.nn.sigmoid(_dot(xb, w4_ref[:, col(3)]))
        merged_ref[:, sl] = (m_a * y_a + m_b * y_b).astype(BF16)
    z = alpha * x + _dot(merged_ref[...], wo_ref[...])
    o_ref[...] = _layer_norm(z, lng_ref[...], lnb_ref[...])


def _merge_call(x2d, hf, hb, rf, rb, w4, g_a, g_b, wo, ln_g, ln_b, alpha):
    tokens = x2d.shape[0]
    tm = MERGE_TM
    tok = pl.BlockSpec((tm, D_MODEL), lambda i: (i, 0))
    flat = lambda a: a.reshape(tokens, V_W)
    return pl.pallas_call(
        functools.partial(_merge_kernel, alpha=alpha),
        grid=(tokens // tm,),
        in_specs=[tok, tok, tok, tok, tok, _resident(w4.shape), _resident(g_a.shape), _resident(g_b.shape),
                  _resident(wo.shape), _resident(ln_g.shape), _resident(ln_b.shape)],
        out_specs=tok,
        out_shape=jax.ShapeDtypeStruct((tokens, D_MODEL), F32),
        scratch_shapes=[pltpu.VMEM((tm, V_W), BF16)],
        compiler_params=pltpu.CompilerParams(dimension_semantics=("parallel",),
                                             vmem_limit_bytes=VMEM_LIMIT),
        name="merge",
    )(x2d, flat(hf), flat(hb), flat(rf), flat(rb), w4, g_a, g_b, wo, ln_g, ln_b)


def _ffn_kernel(x_ref, xp_ref, xn_ref, wup_ref, cw_ref, cb_ref, wdn_ref, lng_ref, lnb_ref, o_ref,
                acc_ref, hv_ref, hg_ref, *, alpha, tiles_per_seq):
    tm = FFN_TM
    rows = tm + 2 * HALO
    n_slabs = FFN_TF // LANES
    pos = lax.rem(pl.program_id(0), tiles_per_seq)
    x = x_ref[...]
    x_prev = jnp.where(pos == 0, 0.0, xp_ref[...])
    x_next = jnp.where(pos == tiles_per_seq - 1, 0.0, xn_ref[...])
    xe = jnp.concatenate([x_prev, x, x_next], axis=0).astype(BF16)

    def put(h_ref, slot, hu):
        for c in range(n_slabs):
            h_ref[slot, c, pl.ds(0, rows, stride=2), :] = hu[:, c * LANES:(c + 1) * LANES]

    def up(j):
        c_val = j * FFN_TF
        c_gate = D_FF + j * FFN_TF
        put(hv_ref, j % FFN_SLOTS, _dot(xe, wup_ref[:, c_val:c_val + FFN_TF]))
        put(hg_ref, j % FFN_SLOTS, _dot(xe, wup_ref[:, c_gate:c_gate + FFN_TF]))

    def conv(h_ref, slot, c0):
        out = []
        for c in range(n_slabs):
            cols = slice(c0 + c * LANES, c0 + (c + 1) * LANES)
            w = cw_ref[:, cols]
            taps = [h_ref[slot, c, pl.ds(2 * (HALO - 1 + k), tm, stride=2), :] for k in range(3)]
            out.append(w[0:1] * taps[0] + w[1:2] * taps[1] + w[2:3] * taps[2] + cb_ref[:, cols])
        return jnp.concatenate(out, axis=1)

    n_chunks = D_FF // FFN_TF
    for j in range(FFN_SLOTS - 1):
        up(j)
    for j in range(n_chunks):
        if j + FFN_SLOTS - 1 < n_chunks:
            up(j + FFN_SLOTS - 1)
        c_val = j * FFN_TF
        slot = j % FFN_SLOTS
        f = (jax.nn.gelu(conv(hg_ref, slot, D_FF + c_val)) * conv(hv_ref, slot, c_val)).astype(BF16)
        part = _dot(f, wdn_ref[c_val:c_val + FFN_TF, :])
        if j == 0:
            acc_ref[...] = part
        else:
            acc_ref[...] += part
    o_ref[...] = _layer_norm(alpha * x + acc_ref[...], lng_ref[...], lnb_ref[...])


def _ffn_call(x2d, seq, wup, cw, cb, wdn, ln_g, ln_b, alpha):
    tokens = x2d.shape[0]
    tm = FFN_TM
    per = tm // HALO
    last = tokens // HALO - 1
    tok = pl.BlockSpec((tm, D_MODEL), lambda i: (i, 0))
    return pl.pallas_call(
        functools.partial(_ffn_kernel, alpha=alpha, tiles_per_seq=seq // tm),
        grid=(tokens // tm,),
        in_specs=[tok,
                  pl.BlockSpec((HALO, D_MODEL), lambda i: (jnp.maximum(i * per - 1, 0), 0)),
                  pl.BlockSpec((HALO, D_MODEL), lambda i: (jnp.minimum((i + 1) * per, last), 0)),
                  _resident(wup.shape), _resident(cw.shape), _resident(cb.shape), _resident(wdn.shape),
                  _resident(ln_g.shape), _resident(ln_b.shape)],
        out_specs=tok,
        out_shape=jax.ShapeDtypeStruct((tokens, D_MODEL), F32),
        scratch_shapes=[pltpu.VMEM((tm, D_MODEL), F32),
                        pltpu.VMEM((FFN_SLOTS, FFN_TF // LANES, 2 * (tm + 2 * HALO), LANES), F32),
                        pltpu.VMEM((FFN_SLOTS, FFN_TF // LANES, 2 * (tm + 2 * HALO), LANES), F32)],
        compiler_params=pltpu.CompilerParams(dimension_semantics=("parallel",),
                                             vmem_limit_bytes=VMEM_LIMIT),
        name="ffn",
    )(x2d, x2d, x2d, wup, cw, cb, wdn, ln_g, ln_b)


def _rope_tables(seq):
    inv_freq = ROPE_BASE ** (-jnp.arange(0, B_QK, 2, dtype=F32) / B_QK)
    ang = jnp.arange(seq, dtype=F32)[:, None] * inv_freq[None, :]
    cos = jnp.cos(ang)
    sin = jnp.sin(ang)
    return cos, sin, cos.T, sin.T


def _layer_weights(l, w_in, b_if, w_o, w_up, w_down):
    w = w_in[l]
    cols = lambda o, n: w[:, o:o + n]
    wtok = jnp.concatenate([cols(O_QA, A_QK_W), cols(O_VA, V_W), cols(O_QB, B_QK_W), cols(O_VB, V_W)],
                           axis=1).astype(BF16)
    wt = jnp.concatenate([cols(O_KA, A_QK_W), cols(O_KB, B_QK_W)], axis=1).T.astype(BF16)
    wg = cols(O_G, N_GATES).T.astype(BF16)
    bg = b_if[l].astype(F32).reshape(N_GATES, 1)
    w4 = jnp.concatenate([cols(O_OA, V_W), cols(O_GB, V_W), cols(O_MA, D_MODEL), cols(O_MB, D_MODEL)],
                         axis=1).astype(BF16)
    return wtok, wt, wg, bg, w4, w_o[l].astype(BF16), w_up[l].astype(BF16), w_down[l].astype(BF16)


def _trunk(x, ln_in, layers, alpha):
    batch, seq, _ = x.shape
    assert seq % PROJ_TM == 0 and seq % FFN_TM == 0 and seq % (MIX_SUB * MIX_L) == 0
    rope = _rope_tables(seq)
    x2d = x.reshape(batch * seq, D_MODEL)
    for l, lw in enumerate(layers):
        (wtok, wt, wg, bg, w4, wo, wup, wdn, lg, g_a, g_b, ln1, cw, cb, ln2) = lw
        outs = _proj_call(x2d, batch, seq, ln_in if l == 0 else None, wtok, wt, wg, bg, rope)
        qa, va, qb, vb, kt, gt = outs[:6]
        if l == 0:
            x2d = outs[6]
        hf, hb, rf, rb = _mixer_call(lg, qa, va, qb, vb, kt, _gates_call(gt), batch, seq)
        x2d = _merge_call(x2d, hf, hb, rf, rb, w4, g_a, g_b, wo, ln1[0], ln1[1], alpha)
        x2d = _ffn_call(x2d, seq, wup, cw, cb, wdn, ln2[0], ln2[1], alpha)
    return x2d.reshape(batch, seq, D_MODEL)


def kernel(x_prompt, x_sample, ln_in_g, ln_in_b, w_in, b_if, ret_log_decay, mlstm_norm_g, ret_norm_g, w_o, ln1_g, ln1_b, w_up, conv_w, conv_b, w_down, ln2_g, ln2_b):
    depth = w_in.shape[0]
    alpha = (2 * depth) ** 0.25
    row = lambda a: a.astype(F32).reshape(1, -1)
    layers = []
    for l in range(depth):
        layers.append(_layer_weights(l, w_in, b_if, w_o, w_up, w_down) + (
            ret_log_decay[l].astype(F32), row(mlstm_norm_g[l]), row(ret_norm_g[l]),
            (row(ln1_g[l]), row(ln1_b[l])), conv_w[l].astype(F32), row(conv_b[l]),
            (row(ln2_g[l]), row(ln2_b[l]))))
    ln_in = (row(ln_in_g), row(ln_in_b))
    return (_trunk(x_prompt, ln_in, layers, alpha), _trunk(x_sample, ln_in, layers, alpha))
```

```python
import functools

import jax
import jax.numpy as jnp
from jax import lax
from jax.experimental import pallas as pl
from jax.experimental.pallas import tpu as pltpu

F32 = jnp.float32
BF16 = jnp.bfloat16

D_MODEL = 1024
HEADS = 4
A_QK = 128
HEAD_V = 256
B_QK = 256
D_FF = 2816
MIX_L = 256
MIX_SUB = 2
LANES = 128
LOG2E = 1.4426950408889634
LN_EPS = 1e-5
ROPE_BASE = 10000.0

A_QK_W = HEADS * A_QK
V_W = HEADS * HEAD_V
B_QK_W = HEADS * B_QK
N_GATES = 16
N_GROWS = 24
KT_ROWS = A_QK_W + B_QK_W

_SPLITS = (A_QK_W, A_QK_W, V_W, V_W, N_GATES, B_QK_W, B_QK_W, V_W, V_W, D_MODEL, D_MODEL)
_OFF = [0]
for _w in _SPLITS:
    _OFF.append(_OFF[-1] + _w)
(O_QA, O_KA, O_VA, O_OA, O_G, O_QB, O_KB, O_VB, O_GB, O_MA, O_MB, _) = _OFF

PROJ_TM = 1024
MERGE_TM = 256
MIX_OUT_DTYPE = jnp.float32
FFN_TM = 512
FFN_TF = 256
FFN_SLOTS = 4
HALO = 8
VMEM_LIMIT = 56 * 1024 * 1024

_NT = (((1,), (1,)), ((), ()))


def _dot(a, b):
    return jnp.dot(a, b, preferred_element_type=F32)


def _dot_nt(a, b):
    return lax.dot_general(a, b, _NT, preferred_element_type=F32)


def _layer_norm(x, g, b):
    mu = jnp.mean(x, axis=-1, keepdims=True)
    xc = x - mu
    var = jnp.mean(xc * xc, axis=-1, keepdims=True)
    return xc * lax.rsqrt(var + LN_EPS) * g + b


def _resident(shape):
    nd = len(shape)
    return pl.BlockSpec(shape, lambda *_: (0,) * nd, pipeline_mode=pl.Buffered(1))


def _scan_chunks(x, op, identity, reverse):
    per_chunk = MIX_L // LANES
    blocks = [x[:, j * LANES:(j + 1) * LANES] for j in range(x.shape[1] // LANES)]
    lane = lax.broadcasted_iota(jnp.int32, blocks[0].shape, 1)
    for j, y in enumerate(blocks):
        k = 1
        while k < LANES:
            if reverse:
                y = op(y, jnp.where(lane < LANES - k, pltpu.roll(y, LANES - k, axis=1), identity))
            else:
                y = op(y, jnp.where(lane >= k, pltpu.roll(y, k, axis=1), identity))
            k *= 2
        blocks[j] = y
    for c0 in range(0, len(blocks), per_chunk):
        if reverse:
            for j in range(c0 + per_chunk - 2, c0 - 1, -1):
                blocks[j] = op(blocks[j], blocks[j + 1][:, 0:1])
        else:
            for j in range(c0 + 1, c0 + per_chunk):
                blocks[j] = op(blocks[j], blocks[j - 1][:, LANES - 1:LANES])
    return jnp.concatenate(blocks, axis=1)


def _proj_kernel(*refs, apply_ln):
    if apply_ln:
        (x_ref, lng_ref, lnb_ref, wtok_ref, wt_ref, wg_ref, bg_ref, cos_ref, sin_ref, cost_ref, sint_ref,
         qa_ref, va_ref, qb_ref, vb_ref, kt_ref, gt_ref, xn_ref) = refs
        x = _layer_norm(x_ref[...], lng_ref[...], lnb_ref[...])
        xn_ref[...] = x
    else:
        (x_ref, wtok_ref, wt_ref, wg_ref, bg_ref, cos_ref, sin_ref, cost_ref, sint_ref,
         qa_ref, va_ref, qb_ref, vb_ref, kt_ref, gt_ref) = refs
        x = x_ref[...]
    xb = x.astype(BF16)

    qa_ref[...] = (_dot(xb, wtok_ref[:, 0:A_QK_W]) * (A_QK ** -0.5)).astype(BF16)
    half = V_W // 2
    for j in range(2):
        c0 = A_QK_W + j * half
        va_ref[:, j * half:(j + 1) * half] = _dot(xb, wtok_ref[:, c0:c0 + half]).astype(BF16)
    cos = cos_ref[...]
    sin = sin_ref[...]
    hq = B_QK // 2
    for h in range(HEADS):
        c0 = A_QK_W + V_W + h * B_QK
        t = _dot(xb, wtok_ref[:, c0:c0 + B_QK])
        t1 = t[:, :hq]
        t2 = t[:, hq:]
        qb_ref[:, h * B_QK:h * B_QK + hq] = (t1 * cos - t2 * sin).astype(BF16)
        qb_ref[:, h * B_QK + hq:(h + 1) * B_QK] = (t1 * sin + t2 * cos).astype(BF16)
    for j in range(2):
        c0 = A_QK_W + V_W + B_QK_W + j * half
        vb_ref[:, j * half:(j + 1) * half] = _dot(xb, wtok_ref[:, c0:c0 + half]).astype(BF16)

    for j in range(2):
        r0 = j * 256
        kt_ref[r0:r0 + 256, :] = _dot_nt(wt_ref[r0:r0 + 256, :], xb).astype(BF16)
    cost = cost_ref[...]
    sint = sint_ref[...]
    for h in range(HEADS):
        r0 = A_QK_W + h * B_QK
        t = _dot_nt(wt_ref[r0:r0 + B_QK, :], xb)
        t1 = t[:hq, :]
        t2 = t[hq:, :]
        kt_ref[r0:r0 + hq, :] = ((t1 * cost - t2 * sint) * (B_QK ** -0.5)).astype(BF16)
        kt_ref[r0 + hq:r0 + B_QK, :] = ((t1 * sint + t2 * cost) * (B_QK ** -0.5)).astype(BF16)

    g = _dot_nt(wg_ref[...], xb) + bg_ref[...]
    row = lax.broadcasted_iota(jnp.int32, g.shape, 0)
    log_sig = jnp.minimum(g, 0.0) - jnp.log1p(jnp.exp(-jnp.abs(g)))
    gt_ref[...] = jnp.where((row & HEADS) != 0, log_sig, g)


def _gates_kernel(g_ref, o_ref):
    g = g_ref[...]
    first4 = lax.broadcasted_iota(jnp.int32, (8, g.shape[1]), 0) < HEADS
    r_pair, cb_pair, cmr_pair = [], [], []
    for d in range(2):
        gd = g[d * 8:(d + 1) * 8, :]
        cums = _scan_chunks(gd, jnp.add, 0.0, reverse=(d == 1))
        r = gd - pltpu.roll(cums, HEADS, axis=0)
        r_pair.append(r)
        cb_pair.append(cums)
        cmr_pair.append(_scan_chunks(r, jnp.maximum, -jnp.inf, reverse=(d == 1)))
    o_ref[0:8, :] = jnp.where(first4, r_pair[0], pltpu.roll(r_pair[1], HEADS, axis=0))
    o_ref[8:16, :] = jnp.where(first4, pltpu.roll(cb_pair[0], HEADS, axis=0), cb_pair[1])
    o_ref[16:24, :] = jnp.where(first4, cmr_pair[0], pltpu.roll(cmr_pair[1], HEADS, axis=0))


def _gates_call(gt):
    batch, _, seq = gt.shape
    return pl.pallas_call(
        _gates_kernel,
        grid=(batch,),
        in_specs=[pl.BlockSpec((None, N_GATES, seq), lambda b: (b, 0, 0))],
        out_specs=pl.BlockSpec((None, N_GROWS, seq), lambda b: (b, 0, 0)),
        out_shape=jax.ShapeDtypeStruct((batch, N_GROWS, seq), F32),
        compiler_params=pltpu.CompilerParams(dimension_semantics=("parallel",),
                                             vmem_limit_bytes=VMEM_LIMIT),
        name="gates",
    )(gt)


def _proj_call(x2d, batch, seq, ln, wtok, wt, wg, bg, rope):
    tokens = x2d.shape[0]
    tm = PROJ_TM
    nt = seq // tm
    cos, sin, cost, sint = rope
    tok = lambda w: pl.BlockSpec((tm, w), lambda i: (i, 0))
    in_specs = [tok(D_MODEL)]
    args = [x2d]
    if ln is not None:
        in_specs += [_resident((1, D_MODEL)), _resident((1, D_MODEL))]
        args += list(ln)
    in_specs += [_resident(wtok.shape), _resident(wt.shape), _resident(wg.shape), _resident(bg.shape),
                 pl.BlockSpec((tm, B_QK // 2), lambda i: (i % nt, 0)),
                 pl.BlockSpec((tm, B_QK // 2), lambda i: (i % nt, 0)),
                 pl.BlockSpec((B_QK // 2, tm), lambda i: (0, i % nt)),
                 pl.BlockSpec((B_QK // 2, tm), lambda i: (0, i % nt))]
    args += [wtok, wt, wg, bg, cos, sin, cost, sint]
    out_shape = [jax.ShapeDtypeStruct((tokens, A_QK_W), BF16),
                 jax.ShapeDtypeStruct((tokens, V_W), BF16),
                 jax.ShapeDtypeStruct((tokens, B_QK_W), BF16),
                 jax.ShapeDtypeStruct((tokens, V_W), BF16),
                 jax.ShapeDtypeStruct((batch, KT_ROWS, seq), BF16),
                 jax.ShapeDtypeStruct((batch, N_GATES, seq), F32)]
    out_specs = [tok(A_QK_W), tok(V_W), tok(B_QK_W), tok(V_W),
                 pl.BlockSpec((None, KT_ROWS, tm), lambda i: (i // nt, 0, i % nt)),
                 pl.BlockSpec((None, N_GATES, tm), lambda i: (i // nt, 0, i % nt))]
    if ln is not None:
        out_shape.append(jax.ShapeDtypeStruct((tokens, D_MODEL), F32))
        out_specs.append(tok(D_MODEL))
    return pl.pallas_call(
        functools.partial(_proj_kernel, apply_ln=ln is not None),
        grid=(tokens // tm,),
        in_specs=in_specs,
        out_specs=out_specs,
        out_shape=out_shape,
        compiler_params=pltpu.CompilerParams(dimension_semantics=("parallel",),
                                             vmem_limit_bytes=VMEM_LIMIT),
        name="proj",
    )(*args)


def _mixer_kernel(lg_ref,
                  qa_f, va_f, qb_f, vb_f, kt_f, g_f,
                  qa_b, va_b, qb_b, vb_b, kt_b, g_b,
                  hf_ref, hb_ref, rf_ref, rb_ref,
                  c_ref, nv_ref, m_ref, r_ref, dm_ref, wi_ref, rc_ref):
    L = MIX_L
    t_i = lax.broadcasted_iota(jnp.int32, (L, L), 0)
    s_i = lax.broadcasted_iota(jnp.int32, (L, L), 1)

    @pl.when(pl.program_id(1) == 0)
    def _init():
        c_ref[...] = jnp.zeros_like(c_ref)
        nv_ref[...] = jnp.zeros_like(nv_ref)
        m_ref[...] = jnp.zeros_like(m_ref)
        r_ref[...] = jnp.zeros_like(r_ref)
        tcol = lax.broadcasted_iota(jnp.int32, (L, LANES), 0).astype(F32)
        srow = lax.broadcasted_iota(jnp.int32, (8, L), 1).astype(F32)
        row8 = lax.broadcasted_iota(jnp.int32, (8, L), 0)
        for d in range(2):
            diff = ((t_i - s_i) if d == 0 else (s_i - t_i)).astype(F32)
            for h in range(HEADS):
                lg = lg_ref[d, h]
                dm_ref[d * HEADS + h] = jnp.where(diff >= 0.0, jnp.exp(lg * jnp.maximum(diff, 0.0)), 0.0)
                if d == 0:
                    wi = jnp.exp(lg * (tcol + 1.0))
                    ws = jnp.exp(lg * (L - 1.0 - srow))
                else:
                    wi = jnp.exp(lg * (L - tcol))
                    ws = jnp.exp(lg * srow)
                wi_ref[d * HEADS + h] = wi
                chunk_dec = jnp.exp(jnp.full((8, L), lg * float(L), F32))
                rc_ref[d * HEADS + h] = jnp.where(row8 == 0, ws, chunk_dec)

    for sub in range(MIX_SUB):
        views = []
        for off, refs in ((sub * L, (qa_f, va_f, qb_f, vb_f, kt_f, g_f, hf_ref, rf_ref)),
                          ((MIX_SUB - 1 - sub) * L, (qa_b, va_b, qb_b, vb_b, kt_b, g_b, hb_ref, rb_ref))):
            qa, va, qb, vb, kt, g, h_out, r_out = refs
            tok = lambda ref: ref.at[off:off + L, :]
            views.append((tok(qa), tok(va), tok(qb), tok(vb), kt.at[:, off:off + L], g.at[:, off:off + L],
                          tok(h_out), tok(r_out)))
        _mixer_chunk(tuple(views), c_ref, nv_ref, m_ref, r_ref, dm_ref, wi_ref, rc_ref)


def _mixer_chunk(dirs, c_ref, nv_ref, m_ref, r_ref, dm_ref, wi_ref, rc_ref):
    L = MIX_L
    eye = (lax.broadcasted_iota(jnp.int32, (A_QK, LANES), 0)
           == lax.broadcasted_iota(jnp.int32, (A_QK, LANES), 1))
    bodies = [(d, h) for d in range(2) for h in range(HEADS)]

    def operands(d, h):
        qa, va, qb, vb, kt = dirs[d][:5]
        return dict(
            q_a=qa[:, h * A_QK:(h + 1) * A_QK],
            kt_a=kt[h * A_QK:(h + 1) * A_QK, :],
            v_a=va[:, h * HEAD_V:(h + 1) * HEAD_V],
            q_b=qb[:, h * B_QK:(h + 1) * B_QK],
            kt_b=kt[A_QK_W + h * B_QK:A_QK_W + (h + 1) * B_QK, :],
            v_b=vb[:, h * HEAD_V:(h + 1) * HEAD_V])


    prep = {}
    for d, h in bodies:
        idx = d * HEADS + h
        g = dirs[d][5]
        end = 0 if d == 1 else L - 1
        r_row = g[idx:idx + 1, :]
        cb_row = g[8 + idx:8 + idx + 1, :]
        cmr_row = g[16 + idx:16 + idx + 1, :]
        cb_col = jnp.transpose(cb_row)
        cmr_col = jnp.transpose(cmr_row)
        m_old = m_ref[idx][0:1, 0:1]
        big_m = jnp.maximum(m_old, cmr_col)
        m_top = jnp.maximum(m_old, cmr_row[:, end:end + 1])
        prep[(d, h)] = dict(
            r2_row=r_row * LOG2E, big_m2=big_m * LOG2E,
            w_inter=jnp.exp(m_old - big_m), floor=jnp.exp(-(cb_col + big_m)),
            w_s=jnp.exp(r_row - m_top), dec=jnp.exp(m_old - m_top),
            m_new=cb_row[:, end:end + 1] + m_top)

    half = L // 2
    t_h = lax.broadcasted_iota(jnp.int32, (half, half), 0)
    s_h = lax.broadcasted_iota(jnp.int32, (half, half), 1)
    st2 = {}
    for d, h in bodies:
        idx = d * HEADS + h
        o = operands(d, h)
        pr = prep[(d, h)]
        tri = (s_h >= t_h) if d == 1 else (s_h <= t_h)
        nr = slice(half, L) if d == 1 else slice(0, half)
        wr = slice(0, half) if d == 1 else slice(half, L)
        r2, bm = pr["r2_row"], pr["big_m2"]
        e_n = jnp.where(tri, jnp.exp2(r2[:, nr] - bm[nr]), 0.0)
        e_w_diag = jnp.where(tri, jnp.exp2(r2[:, wr] - bm[wr]), 0.0)
        e_w_full = jnp.exp2(r2[:, nr] - bm[wr])
        e_w = jnp.concatenate([e_w_diag, e_w_full] if d == 1 else [e_w_full, e_w_diag], axis=1)
        p_n = _dot(o["q_a"][nr], o["kt_a"][:, nr]) * e_n
        p_w = _dot(o["q_a"][wr], o["kt_a"]) * e_w
        dm = dm_ref[idx]
        a_n = _dot(o["q_b"][nr], o["kt_b"][:, nr]) * dm[nr, nr]
        a_w = _dot(o["q_b"][wr], o["kt_b"]) * dm[wr, :]
        kw_t = o["kt_a"].astype(F32) * pr["w_s"]
        kbw_t = o["kt_b"].astype(F32) * rc_ref[idx][0:1, :]
        st2[(d, h)] = dict(
            blocks=((nr, nr, p_n.astype(BF16), jnp.sum(p_n, axis=1, keepdims=True), a_n.astype(BF16)),
                    (wr, slice(0, L), p_w.astype(BF16), jnp.sum(p_w, axis=1, keepdims=True), a_w.astype(BF16))),
            kw=kw_t.astype(BF16), kwsum=jnp.sum(kw_t, axis=1, keepdims=True), kbw=kbw_t.astype(BF16))

    for d, h in bodies:
        idx = d * HEADS + h
        o = operands(d, h)
        pr, s2 = prep[(d, h)], st2[(d, h)]
        h_out, r_out = dirs[d][6], dirs[d][7]
        c_old = c_ref[idx]
        r_old = r_ref[idx]
        nv_old = nv_ref[idx]
        wi = wi_ref[idx]
        wi2 = jnp.concatenate([wi, wi], axis=1)
        nv_row = jnp.sum(jnp.where(eye, nv_old, 0.0), axis=0, keepdims=True)
        q_nv = jnp.sum(o["q_a"].astype(F32) * nv_row, axis=1, keepdims=True)
        qc = _dot(o["q_a"], c_old.astype(BF16))
        qr = _dot(o["q_b"], r_old.astype(BF16))
        for rows, cols, pb, psum, ab in s2["blocks"]:
            den = psum + pr["w_inter"][rows] * q_nv[rows]
            inv = 1.0 / jnp.maximum(jnp.abs(den), pr["floor"][rows])
            num = _dot(pb, o["v_a"][cols]) + pr["w_inter"][rows] * qc[rows]
            h_out[rows, h * HEAD_V:(h + 1) * HEAD_V] = (num * inv).astype(h_out.dtype)
            r_out[rows, h * HEAD_V:(h + 1) * HEAD_V] = (
                _dot(ab, o["v_b"][cols]) + wi2[rows] * qr[rows]).astype(r_out.dtype)
        c_ref[idx] = pr["dec"] * c_old + _dot(s2["kw"], o["v_a"])
        nv_ref[idx] = pr["dec"] * nv_old + s2["kwsum"]
        m_ref[idx] = jnp.broadcast_to(pr["m_new"], m_ref.shape[1:])
        r_ref[idx] = rc_ref[idx][1:2, 0:1] * r_old + _dot(s2["kbw"], o["v_b"])


def _mixer_call(lg, qa, va, qb, vb, kt, gt, batch, seq):
    L = MIX_L
    step = MIX_SUB * L
    n = seq // step
    qa, va, qb, vb = (a.reshape(batch, seq, a.shape[-1]) for a in (qa, va, qb, vb))

    def specs(block_of):
        tokm = lambda w: pl.BlockSpec((None, step, w), lambda b, c: (b, block_of(c), 0))
        featm = lambda r: pl.BlockSpec((None, r, step), lambda b, c: (b, 0, block_of(c)))
        return [tokm(A_QK_W), tokm(V_W), tokm(B_QK_W), tokm(V_W), featm(KT_ROWS), featm(N_GROWS)]

    fwd = lambda c: c
    bwd = lambda c: n - 1 - c
    out_f = pl.BlockSpec((None, step, V_W), lambda b, c: (b, c, 0))
    out_b = pl.BlockSpec((None, step, V_W), lambda b, c: (b, n - 1 - c, 0))
    out_sds = jax.ShapeDtypeStruct((batch, seq, V_W), MIX_OUT_DTYPE)
    nstate = 2 * HEADS
    return pl.pallas_call(
        _mixer_kernel,
        grid=(batch, n),
        in_specs=[pl.BlockSpec(memory_space=pltpu.SMEM)] + specs(fwd) + specs(bwd),
        out_specs=[out_f, out_b, out_f, out_b],
        out_shape=[out_sds] * 4,
        scratch_shapes=[pltpu.VMEM((nstate, A_QK, HEAD_V), F32),
                        pltpu.VMEM((nstate, A_QK, LANES), F32),
                        pltpu.VMEM((nstate, 8, LANES), F32),
                        pltpu.VMEM((nstate, B_QK, HEAD_V), F32),
                        pltpu.VMEM((nstate, L, L), F32),
                        pltpu.VMEM((nstate, L, LANES), F32),
                        pltpu.VMEM((nstate, 8, L), F32)],
        compiler_params=pltpu.CompilerParams(dimension_semantics=("parallel", "arbitrary"),
                                             vmem_limit_bytes=VMEM_LIMIT),
        name="mixer",
    )(lg, qa, va, qb, vb, kt, gt, qa, va, qb, vb, kt, gt)


def _head_norm(h):
    mu = jnp.mean(h, axis=-1, keepdims=True)
    hc = h - mu
    var = jnp.mean(hc * hc, axis=-1, keepdims=True)
    return hc * lax.rsqrt(var + LN_EPS)


def _sigmoid(x):
    return 0.5 * jnp.tanh(0.5 * x) + 0.5


def _merge_kernel(x_ref, hf_ref, hb_ref, rf_ref, rb_ref, w4_ref, ga_ref, gb_ref, wo_ref, lng_ref, lnb_ref,
                  o_ref, merged_ref, *, alpha):
    x = x_ref[...]
    xb = x.astype(BF16)
    for h in range(HEADS):
        sl = slice(h * HEAD_V, (h + 1) * HEAD_V)
        col = lambda piece: slice(piece * V_W + h * HEAD_V, piece * V_W + (h + 1) * HEAD_V)
        o_gate = _sigmoid(_dot(xb, w4_ref[:, col(0)]))
        both = lambda f_ref, b_ref: f_ref[:, sl].astype(F32) + b_ref[:, sl].astype(F32)
        y_a = _head_norm(both(hf_ref, hb_ref) * o_gate) * ga_ref[:, sl]
        g_pre = _dot(xb, w4_ref[:, col(1)])
        y_b = _head_norm(both(rf_ref, rb_ref)) * gb_ref[:, sl] * (g_pre * _sigmoid(g_pre))
        m_a = _sigmoid(_dot(xb, w4_ref[:, col(2)]))
        m_b = _sigmoid(_dot(xb, w4_ref[:, col(3)]))
        merged_ref[:, sl] = (m_a * y_a + m_b * y_b).astype(BF16)
    z = alpha * x + _dot(merged_ref[...], wo_ref[...])
    o_ref[...] = _layer_norm(z, lng_ref[...], lnb_ref[...])


def _merge_call(x2d, hf, hb, rf, rb, w4, g_a, g_b, wo, ln_g, ln_b, alpha):
    tokens = x2d.shape[0]
    tm = MERGE_TM
    tok = pl.BlockSpec((tm, D_MODEL), lambda i: (i, 0))
    flat = lambda a: a.reshape(tokens, V_W)
    return pl.pallas_call(
        functools.partial(_merge_kernel, alpha=alpha),
        grid=(tokens // tm,),
        in_specs=[tok, tok, tok, tok, tok, _resident(w4.shape), _resident(g_a.shape), _resident(g_b.shape),
                  _resident(wo.shape), _resident(ln_g.shape), _resident(ln_b.shape)],
        out_specs=tok,
        out_shape=jax.ShapeDtypeStruct((tokens, D_MODEL), F32),
        scratch_shapes=[pltpu.VMEM((tm, V_W), BF16)],
        compiler_params=pltpu.CompilerParams(dimension_semantics=("parallel",),
                                             vmem_limit_bytes=VMEM_LIMIT),
        name="merge",
    )(x2d, flat(hf), flat(hb), flat(rf), flat(rb), w4, g_a, g_b, wo, ln_g, ln_b)


def _ffn_kernel(x_ref, xp_ref, xn_ref, wup_ref, cw_ref, cb_ref, wdn_ref, lng_ref, lnb_ref, o_ref,
                acc_ref, hv_ref, hg_ref, *, alpha, tiles_per_seq):
    tm = FFN_TM
    rows = tm + 2 * HALO
    n_slabs = FFN_TF // LANES
    pos = lax.rem(pl.program_id(0), tiles_per_seq)
    x = x_ref[...]
    x_prev = jnp.where(pos == 0, 0.0, xp_ref[...])
    x_next = jnp.where(pos == tiles_per_seq - 1, 0.0, xn_ref[...])
    xe = jnp.concatenate([x_prev, x, x_next], axis=0).astype(BF16)

    def put(h_ref, slot, hu):
        for c in range(n_slabs):
            h_ref[slot, c, pl.ds(0, rows, stride=2), :] = hu[:, c * LANES:(c + 1) * LANES]

    def up(j):
        c_val = j * FFN_TF
        c_gate = D_FF + j * FFN_TF
        put(hv_ref, j % FFN_SLOTS, _dot(xe, wup_ref[:, c_val:c_val + FFN_TF]))
        put(hg_ref, j % FFN_SLOTS, _dot(xe, wup_ref[:, c_gate:c_gate + FFN_TF]))

    def conv(h_ref, slot, c0):
        out = []
        for c in range(n_slabs):
            cols = slice(c0 + c * LANES, c0 + (c + 1) * LANES)
            w = cw_ref[:, cols]
            taps = [h_ref[slot, c, pl.ds(2 * (HALO - 1 + k), tm, stride=2), :] for k in range(3)]
            out.append(w[0:1] * taps[0] + w[1:2] * taps[1] + w[2:3] * taps[2] + cb_ref[:, cols])
        return jnp.concatenate(out, axis=1)

    n_chunks = D_FF // FFN_TF
    for j in range(FFN_SLOTS - 1):
        up(j)
    for j in range(n_chunks):
        if j + FFN_SLOTS - 1 < n_chunks:
            up(j + FFN_SLOTS - 1)
        c_val = j * FFN_TF
        slot = j % FFN_SLOTS
        f = (jax.nn.gelu(conv(hg_ref, slot, D_FF + c_val)) * conv(hv_ref, slot, c_val)).astype(BF16)
        part = _dot(f, wdn_ref[c_val:c_val + FFN_TF, :])
        if j == 0:
            acc_ref[...] = part
        else:
            acc_ref[...] += part
    o_ref[...] = _layer_norm(alpha * x + acc_ref[...], lng_ref[...], lnb_ref[...])


def _ffn_call(x2d, seq, wup, cw, cb, wdn, ln_g, ln_b, alpha):
    tokens = x2d.shape[0]
    tm = FFN_TM
    per = tm // HALO
    last = tokens // HALO - 1
    tok = pl.BlockSpec((tm, D_MODEL), lambda i: (i, 0))
    slot_shape = (FFN_SLOTS, FFN_TF // LANES, 2 * (tm + 2 * HALO), LANES)
    return pl.pallas_call(
        functools.partial(_ffn_kernel, alpha=alpha, tiles_per_seq=seq // tm),
        grid=(tokens // tm,),
        in_specs=[tok,
                  pl.BlockSpec((HALO, D_MODEL), lambda i: (jnp.maximum(i * per - 1, 0), 0)),
                  pl.BlockSpec((HALO, D_MODEL), lambda i: (jnp.minimum((i + 1) * per, last), 0)),
                  _resident(wup.shape), _resident(cw.shape), _resident(cb.shape), _resident(wdn.shape),
                  _resident(ln_g.shape), _resident(ln_b.shape)],
        out_specs=tok,
        out_shape=jax.ShapeDtypeStruct((tokens, D_MODEL), F32),
        scratch_shapes=[pltpu.VMEM((tm, D_MODEL), F32),
                        pltpu.VMEM(slot_shape, F32),
                        pltpu.VMEM(slot_shape, F32)],
        compiler_params=pltpu.CompilerParams(dimension_semantics=("parallel",),
                                             vmem_limit_bytes=VMEM_LIMIT),
        name="ffn",
    )(x2d, x2d, x2d, wup, cw, cb, wdn, ln_g, ln_b)


def _rope_tables(seq):
    inv_freq = ROPE_BASE ** (-jnp.arange(0, B_QK, 2, dtype=F32) / B_QK)
    ang = jnp.arange(seq, dtype=F32)[:, None] * inv_freq[None, :]
    cos = jnp.cos(ang)
    sin = jnp.sin(ang)
    return cos, sin, cos.T, sin.T


def _layer_weights(l, w_in, b_if, w_o, w_up, w_down):
    w = w_in[l]
    cols = lambda o, n: w[:, o:o + n]
    wtok = jnp.concatenate([cols(O_QA, A_QK_W), cols(O_VA, V_W), cols(O_QB, B_QK_W), cols(O_VB, V_W)],
                           axis=1).astype(BF16)
    wt = jnp.concatenate([cols(O_KA, A_QK_W), cols(O_KB, B_QK_W)], axis=1).T.astype(BF16)
    wg = cols(O_G, N_GATES).T.astype(BF16)
    bg = b_if[l].astype(F32).reshape(N_GATES, 1)
    w4 = jnp.concatenate([cols(O_OA, V_W), cols(O_GB, V_W), cols(O_MA, D_MODEL), cols(O_MB, D_MODEL)],
                         axis=1).astype(BF16)
    return wtok, wt, wg, bg, w4, w_o[l].astype(BF16), w_up[l].astype(BF16), w_down[l].astype(BF16)


def _trunk(x, ln_in, layers, alpha):
    batch, seq, _ = x.shape
    assert seq % PROJ_TM == 0 and seq % FFN_TM == 0 and seq % (MIX_SUB * MIX_L) == 0
    rope = _rope_tables(seq)
    x2d = x.reshape(batch * seq, D_MODEL)
    for l, lw in enumerate(layers):
        (wtok, wt, wg, bg, w4, wo, wup, wdn, lg, g_a, g_b, ln1, cw, cb, ln2) = lw
        outs = _proj_call(x2d, batch, seq, ln_in if l == 0 else None, wtok, wt, wg, bg, rope)
        qa, va, qb, vb, kt, gt = outs[:6]
        if l == 0:
            x2d = outs[6]
        hf, hb, rf, rb = _mixer_call(lg, qa, va, qb, vb, kt, _gates_call(gt), batch, seq)
        x2d = _merge_call(x2d, hf, hb, rf, rb, w4, g_a, g_b, wo, ln1[0], ln1[1], alpha)
        x2d = _ffn_call(x2d, seq, wup, cw, cb, wdn, ln2[0], ln2[1], alpha)
    return x2d.reshape(batch, seq, D_MODEL)


def kernel(x_prompt, x_sample, ln_in_g, ln_in_b, w_in, b_if, ret_log_decay, mlstm_norm_g, ret_norm_g, w_o, ln1_g, ln1_b, w_up, conv_w, conv_b, w_down, ln2_g, ln2_b):
    depth = w_in.shape[0]
    alpha = (2 * depth) ** 0.25
    row = lambda a: a.astype(F32).reshape(1, -1)
    layers = []
    for l in range(depth):
        layers.append(_layer_weights(l, w_in, b_if, w_o, w_up, w_down) + (
            ret_log_decay[l].astype(F32), row(mlstm_norm_g[l]), row(ret_norm_g[l]),
            (row(ln1_g[l]), row(ln1_b[l])), conv_w[l].astype(F32), row(conv_b[l]),
            (row(ln2_g[l]), row(ln2_b[l]))))
    ln_in = (row(ln_in_g), row(ln_in_b))
    return (_trunk(x_prompt, ln_in, layers, alpha), _trunk(x_sample, ln_in, layers, alpha))
```

```python
import functools

import jax
import jax.numpy as jnp
from jax import lax
from jax.experimental import pallas as pl
from jax.experimental.pallas import tpu as pltpu

F32 = jnp.float32
BF16 = jnp.bfloat16

D_MODEL = 1024
HEADS = 4
A_QK = 128
HEAD_V = 256
B_QK = 256
D_FF = 2816
MIX_L = 256
MIX_SUB = 2
LANES = 128
LOG2E = 1.4426950408889634
LN_EPS = 1e-5
ROPE_BASE = 10000.0

A_QK_W = HEADS * A_QK
V_W = HEADS * HEAD_V
B_QK_W = HEADS * B_QK
N_GATES = 16
N_GROWS = 24
KT_ROWS = A_QK_W + B_QK_W

_SPLITS = (A_QK_W, A_QK_W, V_W, V_W, N_GATES, B_QK_W, B_QK_W, V_W, V_W, D_MODEL, D_MODEL)
_OFF = [0]
for _w in _SPLITS:
    _OFF.append(_OFF[-1] + _w)
(O_QA, O_KA, O_VA, O_OA, O_G, O_QB, O_KB, O_VB, O_GB, O_MA, O_MB, _) = _OFF

PROJ_TM = 1024
GATE_ROWS = 4
MERGE_TM = 256
MIX_OUT_DTYPE = jnp.float32
FFN_TM = 512
FFN_TF = 256
FFN_SLOTS = 4
HALO = 8
VMEM_LIMIT = 56 * 1024 * 1024

_NT = (((1,), (1,)), ((), ()))


def _dot(a, b):
    return jnp.dot(a, b, preferred_element_type=F32)


def _dot_nt(a, b):
    return lax.dot_general(a, b, _NT, preferred_element_type=F32)


def _layer_norm(x, g, b):
    mu = jnp.mean(x, axis=-1, keepdims=True)
    xc = x - mu
    var = jnp.mean(xc * xc, axis=-1, keepdims=True)
    return xc * lax.rsqrt(var + LN_EPS) * g + b


def _resident(shape):
    nd = len(shape)
    return pl.BlockSpec(shape, lambda *_: (0,) * nd, pipeline_mode=pl.Buffered(1))


def _scan_chunks(x, op, identity, reverse):
    per_chunk = MIX_L // LANES
    blocks = [x[:, j * LANES:(j + 1) * LANES] for j in range(x.shape[1] // LANES)]
    lane = lax.broadcasted_iota(jnp.int32, blocks[0].shape, 1)
    for j, y in enumerate(blocks):
        k = 1
        while k < LANES:
            if reverse:
                y = op(y, jnp.where(lane < LANES - k, pltpu.roll(y, LANES - k, axis=1), identity))
            else:
                y = op(y, jnp.where(lane >= k, pltpu.roll(y, k, axis=1), identity))
            k *= 2
        blocks[j] = y
    for c0 in range(0, len(blocks), per_chunk):
        if reverse:
            for j in range(c0 + per_chunk - 2, c0 - 1, -1):
                blocks[j] = op(blocks[j], blocks[j + 1][:, 0:1])
        else:
            for j in range(c0 + 1, c0 + per_chunk):
                blocks[j] = op(blocks[j], blocks[j - 1][:, LANES - 1:LANES])
    return jnp.concatenate(blocks, axis=1)


def _proj_kernel(*refs, apply_ln):
    if apply_ln:
        (x_ref, lng_ref, lnb_ref, wtok_ref, wt_ref, wg_ref, bg_ref, cos_ref, sin_ref, cost_ref, sint_ref,
         qa_ref, va_ref, qb_ref, vb_ref, kt_ref, gt_ref, xn_ref) = refs
        x = _layer_norm(x_ref[...], lng_ref[...], lnb_ref[...])
        xn_ref[...] = x
    else:
        (x_ref, wtok_ref, wt_ref, wg_ref, bg_ref, cos_ref, sin_ref, cost_ref, sint_ref,
         qa_ref, va_ref, qb_ref, vb_ref, kt_ref, gt_ref) = refs
        x = x_ref[...]
    xb = x.astype(BF16)

    qa_ref[...] = (_dot(xb, wtok_ref[:, 0:A_QK_W]) * (A_QK ** -0.5)).astype(BF16)
    half = V_W // 2
    for j in range(2):
        c0 = A_QK_W + j * half
        va_ref[:, j * half:(j + 1) * half] = _dot(xb, wtok_ref[:, c0:c0 + half]).astype(BF16)
    cos = cos_ref[...]
    sin = sin_ref[...]
    hq = B_QK // 2
    for h in range(HEADS):
        c0 = A_QK_W + V_W + h * B_QK
        t = _dot(xb, wtok_ref[:, c0:c0 + B_QK])
        t1 = t[:, :hq]
        t2 = t[:, hq:]
        qb_ref[:, h * B_QK:h * B_QK + hq] = (t1 * cos - t2 * sin).astype(BF16)
        qb_ref[:, h * B_QK + hq:(h + 1) * B_QK] = (t1 * sin + t2 * cos).astype(BF16)
    for j in range(2):
        c0 = A_QK_W + V_W + B_QK_W + j * half
        vb_ref[:, j * half:(j + 1) * half] = _dot(xb, wtok_ref[:, c0:c0 + half]).astype(BF16)

    for j in range(2):
        r0 = j * 256
        kt_ref[r0:r0 + 256, :] = _dot_nt(wt_ref[r0:r0 + 256, :], xb).astype(BF16)
    cost = cost_ref[...]
    sint = sint_ref[...]
    for h in range(HEADS):
        r0 = A_QK_W + h * B_QK
        t = _dot_nt(wt_ref[r0:r0 + B_QK, :], xb)
        t1 = t[:hq, :]
        t2 = t[hq:, :]
        kt_ref[r0:r0 + hq, :] = ((t1 * cost - t2 * sint) * (B_QK ** -0.5)).astype(BF16)
        kt_ref[r0 + hq:r0 + B_QK, :] = ((t1 * sint + t2 * cost) * (B_QK ** -0.5)).astype(BF16)

    g = _dot_nt(wg_ref[...], xb) + bg_ref[...]
    row = lax.broadcasted_iota(jnp.int32, g.shape, 0)
    log_sig = jnp.minimum(g, 0.0) - jnp.log1p(jnp.exp(-jnp.abs(g)))
    gt_ref[...] = jnp.where((row & HEADS) != 0, log_sig, g)


def _gates_kernel(g_ref, o_ref):
    first4 = lax.broadcasted_iota(jnp.int32, (8, g_ref.shape[2]), 0) < HEADS
    for b in range(g_ref.shape[0]):
        g = g_ref[b]
        r_pair, cb_pair, cmr_pair = [], [], []
        for d in range(2):
            gd = g[d * 8:(d + 1) * 8, :]
            cums = _scan_chunks(gd, jnp.add, 0.0, reverse=(d == 1))
            r = gd - pltpu.roll(cums, HEADS, axis=0)
            r_pair.append(r)
            cb_pair.append(cums)
            cmr_pair.append(_scan_chunks(r, jnp.maximum, -jnp.inf, reverse=(d == 1)))
        o_ref[b, 0:8, :] = jnp.where(first4, r_pair[0], pltpu.roll(r_pair[1], HEADS, axis=0))
        o_ref[b, 8:16, :] = jnp.where(first4, pltpu.roll(cb_pair[0], HEADS, axis=0), cb_pair[1])
        o_ref[b, 16:24, :] = jnp.where(first4, cmr_pair[0], pltpu.roll(cmr_pair[1], HEADS, axis=0))


def _gates_call(gt):
    batch, _, seq = gt.shape
    rows = GATE_ROWS if batch % GATE_ROWS == 0 else 1
    return pl.pallas_call(
        _gates_kernel,
        grid=(batch // rows,),
        in_specs=[pl.BlockSpec((rows, N_GATES, seq), lambda b: (b, 0, 0))],
        out_specs=pl.BlockSpec((rows, N_GROWS, seq), lambda b: (b, 0, 0)),
        out_shape=jax.ShapeDtypeStruct((batch, N_GROWS, seq), F32),
        compiler_params=pltpu.CompilerParams(dimension_semantics=("parallel",),
                                             vmem_limit_bytes=VMEM_LIMIT),
        name="gates",
    )(gt)


def _proj_call(x2d, batch, seq, ln, wtok, wt, wg, bg, rope):
    tokens = x2d.shape[0]
    tm = PROJ_TM
    nt = seq // tm
    cos, sin, cost, sint = rope
    tok = lambda w: pl.BlockSpec((tm, w), lambda i: (i, 0))
    in_specs = [tok(D_MODEL)]
    args = [x2d]
    if ln is not None:
        in_specs += [_resident((1, D_MODEL)), _resident((1, D_MODEL))]
        args += list(ln)
    in_specs += [_resident(wtok.shape), _resident(wt.shape), _resident(wg.shape), _resident(bg.shape),
                 pl.BlockSpec((tm, B_QK // 2), lambda i: (i % nt, 0)),
                 pl.BlockSpec((tm, B_QK // 2), lambda i: (i % nt, 0)),
                 pl.BlockSpec((B_QK // 2, tm), lambda i: (0, i % nt)),
                 pl.BlockSpec((B_QK // 2, tm), lambda i: (0, i % nt))]
    args += [wtok, wt, wg, bg, cos, sin, cost, sint]
    out_shape = [jax.ShapeDtypeStruct((tokens, A_QK_W), BF16),
                 jax.ShapeDtypeStruct((tokens, V_W), BF16),
                 jax.ShapeDtypeStruct((tokens, B_QK_W), BF16),
                 jax.ShapeDtypeStruct((tokens, V_W), BF16),
                 jax.ShapeDtypeStruct((batch, KT_ROWS, seq), BF16),
                 jax.ShapeDtypeStruct((batch, N_GATES, seq), F32)]
    out_specs = [tok(A_QK_W), tok(V_W), tok(B_QK_W), tok(V_W),
                 pl.BlockSpec((None, KT_ROWS, tm), lambda i: (i // nt, 0, i % nt)),
                 pl.BlockSpec((None, N_GATES, tm), lambda i: (i // nt, 0, i % nt))]
    if ln is not None:
        out_shape.append(jax.ShapeDtypeStruct((tokens, D_MODEL), F32))
        out_specs.append(tok(D_MODEL))
    return pl.pallas_call(
        functools.partial(_proj_kernel, apply_ln=ln is not None),
        grid=(tokens // tm,),
        in_specs=in_specs,
        out_specs=out_specs,
        out_shape=out_shape,
        compiler_params=pltpu.CompilerParams(dimension_semantics=("parallel",),
                                             vmem_limit_bytes=VMEM_LIMIT),
        name="proj",
    )(*args)


def _mixer_kernel(lg_ref,
                  qa_f, va_f, qb_f, vb_f, kt_f, g_f,
                  qa_b, va_b, qb_b, vb_b, kt_b, g_b,
                  hf_ref, hb_ref, rf_ref, rb_ref,
                  c_ref, nv_ref, m_ref, r_ref, dm_ref, wi_ref, rc_ref):
    L = MIX_L
    t_i = lax.broadcasted_iota(jnp.int32, (L, L), 0)
    s_i = lax.broadcasted_iota(jnp.int32, (L, L), 1)

    @pl.when(pl.program_id(1) == 0)
    def _init():
        c_ref[...] = jnp.zeros_like(c_ref)
        nv_ref[...] = jnp.zeros_like(nv_ref)
        m_ref[...] = jnp.zeros_like(m_ref)
        r_ref[...] = jnp.zeros_like(r_ref)
        tcol = lax.broadcasted_iota(jnp.int32, (L, LANES), 0).astype(F32)
        srow = lax.broadcasted_iota(jnp.int32, (8, L), 1).astype(F32)
        row8 = lax.broadcasted_iota(jnp.int32, (8, L), 0)
        for d in range(2):
            diff = ((t_i - s_i) if d == 0 else (s_i - t_i)).astype(F32)
            for h in range(HEADS):
                lg = lg_ref[d, h]
                dm_ref[d * HEADS + h] = jnp.where(diff >= 0.0, jnp.exp(lg * jnp.maximum(diff, 0.0)), 0.0)
                if d == 0:
                    wi = jnp.exp(lg * (tcol + 1.0))
                    ws = jnp.exp(lg * (L - 1.0 - srow))
                else:
                    wi = jnp.exp(lg * (L - tcol))
                    ws = jnp.exp(lg * srow)
                wi_ref[d * HEADS + h] = wi
                chunk_dec = jnp.exp(jnp.full((8, L), lg * float(L), F32))
                rc_ref[d * HEADS + h] = jnp.where(row8 == 0, ws, chunk_dec)

    for sub in range(MIX_SUB):
        views = []
        for off, refs in ((sub * L, (qa_f, va_f, qb_f, vb_f, kt_f, g_f, hf_ref, rf_ref)),
                          ((MIX_SUB - 1 - sub) * L, (qa_b, va_b, qb_b, vb_b, kt_b, g_b, hb_ref, rb_ref))):
            qa, va, qb, vb, kt, g, h_out, r_out = refs
            tok = lambda ref: ref.at[off:off + L, :]
            views.append((tok(qa), tok(va), tok(qb), tok(vb), kt.at[:, off:off + L], g.at[:, off:off + L],
                          tok(h_out), tok(r_out)))
        _mixer_chunk(tuple(views), c_ref, nv_ref, m_ref, r_ref, dm_ref, wi_ref, rc_ref)


def _mixer_chunk(dirs, c_ref, nv_ref, m_ref, r_ref, dm_ref, wi_ref, rc_ref):
    L = MIX_L
    eye = (lax.broadcasted_iota(jnp.int32, (A_QK, LANES), 0)
           == lax.broadcasted_iota(jnp.int32, (A_QK, LANES), 1))
    bodies = [(d, h) for d in range(2) for h in range(HEADS)]

    def operands(d, h):
        qa, va, qb, vb, kt = dirs[d][:5]
        return dict(
            q_a=qa[:, h * A_QK:(h + 1) * A_QK],
            kt_a=kt[h * A_QK:(h + 1) * A_QK, :],
            v_a=va[:, h * HEAD_V:(h + 1) * HEAD_V],
            q_b=qb[:, h * B_QK:(h + 1) * B_QK],
            kt_b=kt[A_QK_W + h * B_QK:A_QK_W + (h + 1) * B_QK, :],
            v_b=vb[:, h * HEAD_V:(h + 1) * HEAD_V])


    prep = {}
    for d, h in bodies:
        idx = d * HEADS + h
        g = dirs[d][5]
        end = 0 if d == 1 else L - 1
        r_row = g[idx:idx + 1, :]
        cb_row = g[8 + idx:8 + idx + 1, :]
        cmr_row = g[16 + idx:16 + idx + 1, :]
        cb_col = jnp.transpose(cb_row)
        cmr_col = jnp.transpose(cmr_row)
        m_old = m_ref[idx][0:1, 0:1]
        big_m = jnp.maximum(m_old, cmr_col)
        m_top = jnp.maximum(m_old, cmr_row[:, end:end + 1])
        prep[(d, h)] = dict(
            r2_row=r_row * LOG2E, big_m2=big_m * LOG2E,
            w_inter=jnp.exp(m_old - big_m), floor=jnp.exp(-(cb_col + big_m)),
            w_s=jnp.exp(r_row - m_top), dec=jnp.exp(m_old - m_top),
            m_new=cb_row[:, end:end + 1] + m_top)

    half = L // 2
    t_h = lax.broadcasted_iota(jnp.int32, (half, half), 0)
    s_h = lax.broadcasted_iota(jnp.int32, (half, half), 1)
    st2 = {}
    for d, h in bodies:
        idx = d * HEADS + h
        o = operands(d, h)
        pr = prep[(d, h)]
        tri = (s_h >= t_h) if d == 1 else (s_h <= t_h)
        nr = slice(half, L) if d == 1 else slice(0, half)
        wr = slice(0, half) if d == 1 else slice(half, L)
        r2, bm = pr["r2_row"], pr["big_m2"]
        e_n = jnp.where(tri, jnp.exp2(r2[:, nr] - bm[nr]), 0.0)
        e_w_diag = jnp.where(tri, jnp.exp2(r2[:, wr] - bm[wr]), 0.0)
        e_w_full = jnp.exp2(r2[:, nr] - bm[wr])
        e_w = jnp.concatenate([e_w_diag, e_w_full] if d == 1 else [e_w_full, e_w_diag], axis=1)
        p_n = _dot(o["q_a"][nr], o["kt_a"][:, nr]) * e_n
        p_w = _dot(o["q_a"][wr], o["kt_a"]) * e_w
        dm = dm_ref[idx]
        a_n = _dot(o["q_b"][nr], o["kt_b"][:, nr]) * dm[nr, nr]
        a_w = _dot(o["q_b"][wr], o["kt_b"]) * dm[wr, :]
        kw_t = o["kt_a"].astype(F32) * pr["w_s"]
        kbw_t = o["kt_b"].astype(F32) * rc_ref[idx][0:1, :]
        st2[(d, h)] = dict(
            blocks=((nr, nr, p_n.astype(BF16), jnp.sum(p_n, axis=1, keepdims=True), a_n.astype(BF16)),
                    (wr, slice(0, L), p_w.astype(BF16), jnp.sum(p_w, axis=1, keepdims=True), a_w.astype(BF16))),
            kw=kw_t.astype(BF16), kwsum=jnp.sum(kw_t, axis=1, keepdims=True), kbw=kbw_t.astype(BF16))

    for d, h in bodies:
        idx = d * HEADS + h
        o = operands(d, h)
        pr, s2 = prep[(d, h)], st2[(d, h)]
        h_out, r_out = dirs[d][6], dirs[d][7]
        c_old = c_ref[idx]
        r_old = r_ref[idx]
        nv_old = nv_ref[idx]
        wi = wi_ref[idx]
        wi2 = jnp.concatenate([wi, wi], axis=1)
        nv_row = jnp.sum(jnp.where(eye, nv_old, 0.0), axis=0, keepdims=True)
        q_nv = jnp.sum(o["q_a"].astype(F32) * nv_row, axis=1, keepdims=True)
        qc = _dot(o["q_a"], c_old.astype(BF16))
        qr = _dot(o["q_b"], r_old.astype(BF16))
        for rows, cols, pb, psum, ab in s2["blocks"]:
            den = psum + pr["w_inter"][rows] * q_nv[rows]
            inv = 1.0 / jnp.maximum(jnp.abs(den), pr["floor"][rows])
            num = _dot(pb, o["v_a"][cols]) + pr["w_inter"][rows] * qc[rows]
            h_out[rows, h * HEAD_V:(h + 1) * HEAD_V] = (num * inv).astype(h_out.dtype)
            r_out[rows, h * HEAD_V:(h + 1) * HEAD_V] = (
                _dot(ab, o["v_b"][cols]) + wi2[rows] * qr[rows]).astype(r_out.dtype)
        c_ref[idx] = pr["dec"] * c_old + _dot(s2["kw"], o["v_a"])
        nv_ref[idx] = pr["dec"] * nv_old + s2["kwsum"]
        m_ref[idx] = jnp.broadcast_to(pr["m_new"], m_ref.shape[1:])
        r_ref[idx] = rc_ref[idx][1:2, 0:1] * r_old + _dot(s2["kbw"], o["v_b"])


def _mixer_call(lg, qa, va, qb, vb, kt, gt, batch, seq):
    L = MIX_L
    step = MIX_SUB * L
    n = seq // step
    qa, va, qb, vb = (a.reshape(batch, seq, a.shape[-1]) for a in (qa, va, qb, vb))

    def specs(block_of):
        tokm = lambda w: pl.BlockSpec((None, step, w), lambda b, c: (b, block_of(c), 0))
        featm = lambda r: pl.BlockSpec((None, r, step), lambda b, c: (b, 0, block_of(c)))
        return [tokm(A_QK_W), tokm(V_W), tokm(B_QK_W), tokm(V_W), featm(KT_ROWS), featm(N_GROWS)]

    fwd = lambda c: c
    bwd = lambda c: n - 1 - c
    out_f = pl.BlockSpec((None, step, V_W), lambda b, c: (b, c, 0))
    out_b = pl.BlockSpec((None, step, V_W), lambda b, c: (b, n - 1 - c, 0))
    out_sds = jax.ShapeDtypeStruct((batch, seq, V_W), MIX_OUT_DTYPE)
    nstate = 2 * HEADS
    return pl.pallas_call(
        _mixer_kernel,
        grid=(batch, n),
        in_specs=[pl.BlockSpec(memory_space=pltpu.SMEM)] + specs(fwd) + specs(bwd),
        out_specs=[out_f, out_b, out_f, out_b],
        out_shape=[out_sds] * 4,
        scratch_shapes=[pltpu.VMEM((nstate, A_QK, HEAD_V), F32),
                        pltpu.VMEM((nstate, A_QK, LANES), F32),
                        pltpu.VMEM((nstate, 8, LANES), F32),
                        pltpu.VMEM((nstate, B_QK, HEAD_V), F32),
                        pltpu.VMEM((nstate, L, L), F32),
                        pltpu.VMEM((nstate, L, LANES), F32),
                        pltpu.VMEM((nstate, 8, L), F32)],
        compiler_params=pltpu.CompilerParams(dimension_semantics=("parallel", "arbitrary"),
                                             vmem_limit_bytes=VMEM_LIMIT),
        name="mixer",
    )(lg, qa, va, qb, vb, kt, gt, qa, va, qb, vb, kt, gt)


def _head_norm(h):
    mu = jnp.mean(h, axis=-1, keepdims=True)
    hc = h - mu
    var = jnp.mean(hc * hc, axis=-1, keepdims=True)
    return hc * lax.rsqrt(var + LN_EPS)


def _sigmoid(x):
    return 0.5 * jnp.tanh(0.5 * x) + 0.5


def _merge_kernel(x_ref, hf_ref, hb_ref, rf_ref, rb_ref, w4_ref, ga_ref, gb_ref, wo_ref, lng_ref, lnb_ref,
                  o_ref, merged_ref, *, alpha):
    x = x_ref[...]
    xb = x.astype(BF16)
    for h in range(HEADS):
        sl = slice(h * HEAD_V, (h + 1) * HEAD_V)
        col = lambda piece: slice(piece * V_W + h * HEAD_V, piece * V_W + (h + 1) * HEAD_V)
        o_gate = _sigmoid(_dot(xb, w4_ref[:, col(0)]))
        both = lambda f_ref, b_ref: f_ref[:, sl].astype(F32) + b_ref[:, sl].astype(F32)
        y_a = _head_norm(both(hf_ref, hb_ref) * o_gate) * ga_ref[:, sl]
        g_pre = _dot(xb, w4_ref[:, col(1)])
        y_b = _head_norm(both(rf_ref, rb_ref)) * gb_ref[:, sl] * (g_pre * _sigmoid(g_pre))
        m_a = _sigmoid(_dot(xb, w4_ref[:, col(2)]))
        m_b = _sigmoid(_dot(xb, w4_ref[:, col(3)]))
        merged_ref[:, sl] = (m_a * y_a + m_b * y_b).astype(BF16)
    z = alpha * x + _dot(merged_ref[...], wo_ref[...])
    o_ref[...] = _layer_norm(z, lng_ref[...], lnb_ref[...])


def _merge_call(x2d, hf, hb, rf, rb, w4, g_a, g_b, wo, ln_g, ln_b, alpha):
    tokens = x2d.shape[0]
    tm = MERGE_TM
    tok = pl.BlockSpec((tm, D_MODEL), lambda i: (i, 0))
    flat = lambda a: a.reshape(tokens, V_W)
    return pl.pallas_call(
        functools.partial(_merge_kernel, alpha=alpha),
        grid=(tokens // tm,),
        in_specs=[tok, tok, tok, tok, tok, _resident(w4.shape), _resident(g_a.shape), _resident(g_b.shape),
                  _resident(wo.shape), _resident(ln_g.shape), _resident(ln_b.shape)],
        out_specs=tok,
        out_shape=jax.ShapeDtypeStruct((tokens, D_MODEL), F32),
        scratch_shapes=[pltpu.VMEM((tm, V_W), BF16)],
        compiler_params=pltpu.CompilerParams(dimension_semantics=("parallel",),
                                             vmem_limit_bytes=VMEM_LIMIT),
        name="merge",
    )(x2d, flat(hf), flat(hb), flat(rf), flat(rb), w4, g_a, g_b, wo, ln_g, ln_b)


def _ffn_kernel(x_ref, xp_ref, xn_ref, wup_ref, cw_ref, cb_ref, wdn_ref, lng_ref, lnb_ref, o_ref,
                acc_ref, hv_ref, hg_ref, *, alpha, tiles_per_seq):
    tm = FFN_TM
    rows = tm + 2 * HALO
    n_slabs = FFN_TF // LANES
    pos = lax.rem(pl.program_id(0), tiles_per_seq)
    x = x_ref[...]
    x_prev = jnp.where(pos == 0, 0.0, xp_ref[...])
    x_next = jnp.where(pos == tiles_per_seq - 1, 0.0, xn_ref[...])
    xe = jnp.concatenate([x_prev, x, x_next], axis=0).astype(BF16)

    def put(h_ref, slot, hu):
        for c in range(n_slabs):
            h_ref[slot, c, pl.ds(0, rows, stride=2), :] = hu[:, c * LANES:(c + 1) * LANES]

    def up(j):
        c_val = j * FFN_TF
        c_gate = D_FF + j * FFN_TF
        put(hv_ref, j % FFN_SLOTS, _dot(xe, wup_ref[:, c_val:c_val + FFN_TF]))
        put(hg_ref, j % FFN_SLOTS, _dot(xe, wup_ref[:, c_gate:c_gate + FFN_TF]))

    def conv(h_ref, slot, c0):
        out = []
        for c in range(n_slabs):
            cols = slice(c0 + c * LANES, c0 + (c + 1) * LANES)
            w = cw_ref[:, cols]
            taps = [h_ref[slot, c, pl.ds(2 * (HALO - 1 + k), tm, stride=2), :] for k in range(3)]
            out.append(w[0:1] * taps[0] + w[1:2] * taps[1] + w[2:3] * taps[2] + cb_ref[:, cols])
        return jnp.concatenate(out, axis=1)

    n_chunks = D_FF // FFN_TF
    for j in range(FFN_SLOTS - 1):
        up(j)
    for j in range(n_chunks):
        if j + FFN_SLOTS - 1 < n_chunks:
            up(j + FFN_SLOTS - 1)
        c_val = j * FFN_TF
        slot = j % FFN_SLOTS
        f = (jax.nn.gelu(conv(hg_ref, slot, D_FF + c_val)) * conv(hv_ref, slot, c_val)).astype(BF16)
        part = _dot(f, wdn_ref[c_val:c_val + FFN_TF, :])
        if j == 0:
            acc_ref[...] = part
        else:
            acc_ref[...] += part
    o_ref[...] = _layer_norm(alpha * x + acc_ref[...], lng_ref[...], lnb_ref[...])


def _ffn_call(x2d, seq, wup, cw, cb, wdn, ln_g, ln_b, alpha):
    tokens = x2d.shape[0]
    tm = FFN_TM
    per = tm // HALO
    last = tokens // HALO - 1
    tok = pl.BlockSpec((tm, D_MODEL), lambda i: (i, 0))
    slot_shape = (FFN_SLOTS, FFN_TF // LANES, 2 * (tm + 2 * HALO), LANES)
    return pl.pallas_call(
        functools.partial(_ffn_kernel, alpha=alpha, tiles_per_seq=seq // tm),
        grid=(tokens // tm,),
        in_specs=[tok,
                  pl.BlockSpec((HALO, D_MODEL), lambda i: (jnp.maximum(i * per - 1, 0), 0)),
                  pl.BlockSpec((HALO, D_MODEL), lambda i: (jnp.minimum((i + 1) * per, last), 0)),
                  _resident(wup.shape), _resident(cw.shape), _resident(cb.shape), _resident(wdn.shape),
                  _resident(ln_g.shape), _resident(ln_b.shape)],
        out_specs=tok,
        out_shape=jax.ShapeDtypeStruct((tokens, D_MODEL), F32),
        scratch_shapes=[pltpu.VMEM((tm, D_MODEL), F32),
                        pltpu.VMEM(slot_shape, F32),
                        pltpu.VMEM(slot_shape, F32)],
        compiler_params=pltpu.CompilerParams(dimension_semantics=("parallel",),
                                             vmem_limit_bytes=VMEM_LIMIT),
        name="ffn",
    )(x2d, x2d, x2d, wup, cw, cb, wdn, ln_g, ln_b)


def _rope_tables(seq):
    inv_freq = ROPE_BASE ** (-jnp.arange(0, B_QK, 2, dtype=F32) / B_QK)
    ang = jnp.arange(seq, dtype=F32)[:, None] * inv_freq[None, :]
    cos = jnp.cos(ang)
    sin = jnp.sin(ang)
    return cos, sin, cos.T, sin.T


def _layer_weights(l, w_in, b_if, w_o, w_up, w_down):
    w = w_in[l]
    cols = lambda o, n: w[:, o:o + n]
    wtok = jnp.concatenate([cols(O_QA, A_QK_W), cols(O_VA, V_W), cols(O_QB, B_QK_W), cols(O_VB, V_W)],
                           axis=1).astype(BF16)
    wt = jnp.concatenate([cols(O_KA, A_QK_W), cols(O_KB, B_QK_W)], axis=1).T.astype(BF16)
    wg = cols(O_G, N_GATES).T.astype(BF16)
    bg = b_if[l].astype(F32).reshape(N_GATES, 1)
    w4 = jnp.concatenate([cols(O_OA, V_W), cols(O_GB, V_W), cols(O_MA, D_MODEL), cols(O_MB, D_MODEL)],
                         axis=1).astype(BF16)
    return wtok, wt, wg, bg, w4, w_o[l].astype(BF16), w_up[l].astype(BF16), w_down[l].astype(BF16)


def _trunk(x, ln_in, layers, alpha):
    batch, seq, _ = x.shape
    assert seq % PROJ_TM == 0 and seq % FFN_TM == 0 and seq % (MIX_SUB * MIX_L) == 0
    rope = _rope_tables(seq)
    x2d = x.reshape(batch * seq, D_MODEL)
    for l, lw in enumerate(layers):
        (wtok, wt, wg, bg, w4, wo, wup, wdn, lg, g_a, g_b, ln1, cw, cb, ln2) = lw
        outs = _proj_call(x2d, batch, seq, ln_in if l == 0 else None, wtok, wt, wg, bg, rope)
        qa, va, qb, vb, kt, gt = outs[:6]
        if l == 0:
            x2d = outs[6]
        hf, hb, rf, rb = _mixer_call(lg, qa, va, qb, vb, kt, _gates_call(gt), batch, seq)
        x2d = _merge_call(x2d, hf, hb, rf, rb, w4, g_a, g_b, wo, ln1[0], ln1[1], alpha)
        x2d = _ffn_call(x2d, seq, wup, cw, cb, wdn, ln2[0], ln2[1], alpha)
    return x2d.reshape(batch, seq, D_MODEL)


def kernel(x_prompt, x_sample, ln_in_g, ln_in_b, w_in, b_if, ret_log_decay, mlstm_norm_g, ret_norm_g, w_o, ln1_g, ln1_b, w_up, conv_w, conv_b, w_down, ln2_g, ln2_b):
    depth = w_in.shape[0]
    alpha = (2 * depth) ** 0.25
    row = lambda a: a.astype(F32).reshape(1, -1)
    layers = []
    for l in range(depth):
        layers.append(_layer_weights(l, w_in, b_if, w_o, w_up, w_down) + (
            ret_log_decay[l].astype(F32), row(mlstm_norm_g[l]), row(ret_norm_g[l]),
            (row(ln1_g[l]), row(ln1_b[l])), conv_w[l].astype(F32), row(conv_b[l]),
            (row(ln2_g[l]), row(ln2_b[l]))))
    ln_in = (row(ln_in_g), row(ln_in_b))
    return (_trunk(x_prompt, ln_in, layers, alpha), _trunk(x_sample, ln_in, layers, alpha))
```

```python
import functools

import jax
import jax.numpy as jnp
from jax import lax
from jax.experimental import pallas as pl
from jax.experimental.pallas import tpu as pltpu

F32 = jnp.float32
BF16 = jnp.bfloat16

D_MODEL = 1024
HEADS = 4
A_QK = 128
HEAD_V = 256
B_QK = 256
D_FF = 2816
MIX_L = 256
MIX_SUB = 2
LANES = 128
LOG2E = 1.4426950408889634
LN_EPS = 1e-5
ROPE_BASE = 10000.0

A_QK_W = HEADS * A_QK
V_W = HEADS * HEAD_V
B_QK_W = HEADS * B_QK
N_GATES = 16
N_GROWS = 24
KT_ROWS = A_QK_W + B_QK_W

_SPLITS = (A_QK_W, A_QK_W, V_W, V_W, N_GATES, B_QK_W, B_QK_W, V_W, V_W, D_MODEL, D_MODEL)
_OFF = [0]
for _w in _SPLITS:
    _OFF.append(_OFF[-1] + _w)
(O_QA, O_KA, O_VA, O_OA, O_G, O_QB, O_KB, O_VB, O_GB, O_MA, O_MB, _) = _OFF

PROJ_TM = 1024
MERGE_TM = 256
MIX_OUT_DTYPE = jnp.float32
FFN_TM = 512
FFN_TF = 256
FFN_SLOTS = 4
HALO = 8
VMEM_LIMIT = 56 * 1024 * 1024

_NT = (((1,), (1,)), ((), ()))


def _dot(a, b):
    return jnp.dot(a, b, preferred_element_type=F32)


def _dot_nt(a, b):
    return lax.dot_general(a, b, _NT, preferred_element_type=F32)


def _layer_norm(x, g, b):
    mu = jnp.mean(x, axis=-1, keepdims=True)
    xc = x - mu
    var = jnp.mean(xc * xc, axis=-1, keepdims=True)
    return xc * lax.rsqrt(var + LN_EPS) * g + b


def _resident(shape):
    nd = len(shape)
    return pl.BlockSpec(shape, lambda *_: (0,) * nd, pipeline_mode=pl.Buffered(1))


def _scan_chunks(x, op, identity, reverse):
    per_chunk = MIX_L // LANES
    blocks = [x[:, j * LANES:(j + 1) * LANES] for j in range(x.shape[1] // LANES)]
    lane = lax.broadcasted_iota(jnp.int32, blocks[0].shape, 1)
    for j, y in enumerate(blocks):
        k = 1
        while k < LANES:
            if reverse:
                y = op(y, jnp.where(lane < LANES - k, pltpu.roll(y, LANES - k, axis=1), identity))
            else:
                y = op(y, jnp.where(lane >= k, pltpu.roll(y, k, axis=1), identity))
            k *= 2
        blocks[j] = y
    for c0 in range(0, len(blocks), per_chunk):
        if reverse:
            for j in range(c0 + per_chunk - 2, c0 - 1, -1):
                blocks[j] = op(blocks[j], blocks[j + 1][:, 0:1])
        else:
            for j in range(c0 + 1, c0 + per_chunk):
                blocks[j] = op(blocks[j], blocks[j - 1][:, LANES - 1:LANES])
    return jnp.concatenate(blocks, axis=1)


def _proj_kernel(*refs, apply_ln):
    if apply_ln:
        (x_ref, lng_ref, lnb_ref, wtok_ref, wt_ref, wg_ref, bg_ref, cos_ref, sin_ref, cost_ref, sint_ref,
         qa_ref, va_ref, qb_ref, vb_ref, kt_ref, gt_ref, xn_ref) = refs
        x = _layer_norm(x_ref[...], lng_ref[...], lnb_ref[...])
        xn_ref[...] = x
    else:
        (x_ref, wtok_ref, wt_ref, wg_ref, bg_ref, cos_ref, sin_ref, cost_ref, sint_ref,
         qa_ref, va_ref, qb_ref, vb_ref, kt_ref, gt_ref) = refs
        x = x_ref[...]
    xb = x.astype(BF16)

    qa_ref[...] = (_dot(xb, wtok_ref[:, 0:A_QK_W]) * (A_QK ** -0.5)).astype(BF16)
    half = V_W // 2
    for j in range(2):
        c0 = A_QK_W + j * half
        va_ref[:, j * half:(j + 1) * half] = _dot(xb, wtok_ref[:, c0:c0 + half]).astype(BF16)
    cos = cos_ref[...]
    sin = sin_ref[...]
    hq = B_QK // 2
    for h in range(HEADS):
        c0 = A_QK_W + V_W + h * B_QK
        t = _dot(xb, wtok_ref[:, c0:c0 + B_QK])
        t1 = t[:, :hq]
        t2 = t[:, hq:]
        qb_ref[:, h * B_QK:h * B_QK + hq] = (t1 * cos - t2 * sin).astype(BF16)
        qb_ref[:, h * B_QK + hq:(h + 1) * B_QK] = (t1 * sin + t2 * cos).astype(BF16)
    for j in range(2):
        c0 = A_QK_W + V_W + B_QK_W + j * half
        vb_ref[:, j * half:(j + 1) * half] = _dot(xb, wtok_ref[:, c0:c0 + half]).astype(BF16)

    for j in range(2):
        r0 = j * 256
        kt_ref[r0:r0 + 256, :] = _dot_nt(wt_ref[r0:r0 + 256, :], xb).astype(BF16)
    cost = cost_ref[...]
    sint = sint_ref[...]
    for h in range(HEADS):
        r0 = A_QK_W + h * B_QK
        t = _dot_nt(wt_ref[r0:r0 + B_QK, :], xb)
        t1 = t[:hq, :]
        t2 = t[hq:, :]
        kt_ref[r0:r0 + hq, :] = ((t1 * cost - t2 * sint) * (B_QK ** -0.5)).astype(BF16)
        kt_ref[r0 + hq:r0 + B_QK, :] = ((t1 * sint + t2 * cost) * (B_QK ** -0.5)).astype(BF16)

    g = _dot_nt(wg_ref[...], xb) + bg_ref[...]
    row = lax.broadcasted_iota(jnp.int32, g.shape, 0)
    log_sig = jnp.minimum(g, 0.0) - jnp.log1p(jnp.exp(-jnp.abs(g)))
    gt_ref[...] = jnp.where((row & HEADS) != 0, log_sig, g)


def _gates_kernel(g_ref, o_ref):
    g = g_ref[...]
    first4 = lax.broadcasted_iota(jnp.int32, (8, g.shape[1]), 0) < HEADS
    r_pair, cb_pair, cmr_pair = [], [], []
    for d in range(2):
        gd = g[d * 8:(d + 1) * 8, :]
        cums = _scan_chunks(gd, jnp.add, 0.0, reverse=(d == 1))
        r = gd - pltpu.roll(cums, HEADS, axis=0)
        r_pair.append(r)
        cb_pair.append(cums)
        cmr_pair.append(_scan_chunks(r, jnp.maximum, -jnp.inf, reverse=(d == 1)))
    o_ref[0:8, :] = jnp.where(first4, r_pair[0], pltpu.roll(r_pair[1], HEADS, axis=0))
    o_ref[8:16, :] = jnp.where(first4, pltpu.roll(cb_pair[0], HEADS, axis=0), cb_pair[1])
    o_ref[16:24, :] = jnp.where(first4, cmr_pair[0], pltpu.roll(cmr_pair[1], HEADS, axis=0))


def _gates_call(gt):
    batch, _, seq = gt.shape
    return pl.pallas_call(
        _gates_kernel,
        grid=(batch,),
        in_specs=[pl.BlockSpec((None, N_GATES, seq), lambda b: (b, 0, 0))],
        out_specs=pl.BlockSpec((None, N_GROWS, seq), lambda b: (b, 0, 0)),
        out_shape=jax.ShapeDtypeStruct((batch, N_GROWS, seq), F32),
        compiler_params=pltpu.CompilerParams(dimension_semantics=("parallel",),
                                             vmem_limit_bytes=VMEM_LIMIT),
        name="gates",
    )(gt)


def _proj_call(x2d, batch, seq, ln, wtok, wt, wg, bg, rope):
    tokens = x2d.shape[0]
    tm = PROJ_TM
    nt = seq // tm
    cos, sin, cost, sint = rope
    tok = lambda w: pl.BlockSpec((tm, w), lambda i: (i, 0))
    in_specs = [tok(D_MODEL)]
    args = [x2d]
    if ln is not None:
        in_specs += [_resident((1, D_MODEL)), _resident((1, D_MODEL))]
        args += list(ln)
    in_specs += [_resident(wtok.shape), _resident(wt.shape), _resident(wg.shape), _resident(bg.shape),
                 pl.BlockSpec((tm, B_QK // 2), lambda i: (i % nt, 0)),
                 pl.BlockSpec((tm, B_QK // 2), lambda i: (i % nt, 0)),
                 pl.BlockSpec((B_QK // 2, tm), lambda i: (0, i % nt)),
                 pl.BlockSpec((B_QK // 2, tm), lambda i: (0, i % nt))]
    args += [wtok, wt, wg, bg, cos, sin, cost, sint]
    out_shape = [jax.ShapeDtypeStruct((tokens, A_QK_W), BF16),
                 jax.ShapeDtypeStruct((tokens, V_W), BF16),
                 jax.ShapeDtypeStruct((tokens, B_QK_W), BF16),
                 jax.ShapeDtypeStruct((tokens, V_W), BF16),
                 jax.ShapeDtypeStruct((batch, KT_ROWS, seq), BF16),
                 jax.ShapeDtypeStruct((batch, N_GATES, seq), F32)]
    out_specs = [tok(A_QK_W), tok(V_W), tok(B_QK_W), tok(V_W),
                 pl.BlockSpec((None, KT_ROWS, tm), lambda i: (i // nt, 0, i % nt)),
                 pl.BlockSpec((None, N_GATES, tm), lambda i: (i // nt, 0, i % nt))]
    if ln is not None:
        out_shape.append(jax.ShapeDtypeStruct((tokens, D_MODEL), F32))
        out_specs.append(tok(D_MODEL))
    return pl.pallas_call(
        functools.partial(_proj_kernel, apply_ln=ln is not None),
        grid=(tokens // tm,),
        in_specs=in_specs,
        out_specs=out_specs,
        out_shape=out_shape,
        compiler_params=pltpu.CompilerParams(dimension_semantics=("parallel",),
                                             vmem_limit_bytes=VMEM_LIMIT),
        name="proj",
    )(*args)


def _mixer_kernel(lg_ref,
                  qa_f, va_f, qb_f, vb_f, kt_f, g_f,
                  qa_b, va_b, qb_b, vb_b, kt_b, g_b,
                  hf_ref, hb_ref, rf_ref, rb_ref,
                  c_ref, nv_ref, m_ref, r_ref, dm_ref, wi_ref, rc_ref):
    L = MIX_L
    t_i = lax.broadcasted_iota(jnp.int32, (L, L), 0)
    s_i = lax.broadcasted_iota(jnp.int32, (L, L), 1)

    @pl.when(pl.program_id(1) == 0)
    def _init():
        c_ref[...] = jnp.zeros_like(c_ref)
        nv_ref[...] = jnp.zeros_like(nv_ref)
        m_ref[...] = jnp.zeros_like(m_ref)
        r_ref[...] = jnp.zeros_like(r_ref)
        tcol = lax.broadcasted_iota(jnp.int32, (L, LANES), 0).astype(F32)
        srow = lax.broadcasted_iota(jnp.int32, (8, L), 1).astype(F32)
        row8 = lax.broadcasted_iota(jnp.int32, (8, L), 0)
        for d in range(2):
            diff = ((t_i - s_i) if d == 0 else (s_i - t_i)).astype(F32)
            for h in range(HEADS):
                lg = lg_ref[d, h]
                dm_ref[d * HEADS + h] = jnp.where(diff >= 0.0, jnp.exp(lg * jnp.maximum(diff, 0.0)), 0.0)
                if d == 0:
                    wi = jnp.exp(lg * (tcol + 1.0))
                    ws = jnp.exp(lg * (L - 1.0 - srow))
                else:
                    wi = jnp.exp(lg * (L - tcol))
                    ws = jnp.exp(lg * srow)
                wi_ref[d * HEADS + h] = wi
                chunk_dec = jnp.exp(jnp.full((8, L), lg * float(L), F32))
                rc_ref[d * HEADS + h] = jnp.where(row8 == 0, ws, chunk_dec)

    for sub in range(MIX_SUB):
        views = []
        for off, refs in ((sub * L, (qa_f, va_f, qb_f, vb_f, kt_f, g_f, hf_ref, rf_ref)),
                          ((MIX_SUB - 1 - sub) * L, (qa_b, va_b, qb_b, vb_b, kt_b, g_b, hb_ref, rb_ref))):
            qa, va, qb, vb, kt, g, h_out, r_out = refs
            tok = lambda ref: ref.at[off:off + L, :]
            views.append((tok(qa), tok(va), tok(qb), tok(vb), kt.at[:, off:off + L], g.at[:, off:off + L],
                          tok(h_out), tok(r_out)))
        _mixer_chunk(tuple(views), c_ref, nv_ref, m_ref, r_ref, dm_ref, wi_ref, rc_ref)


def _mixer_chunk(dirs, c_ref, nv_ref, m_ref, r_ref, dm_ref, wi_ref, rc_ref):
    L = MIX_L
    eye = (lax.broadcasted_iota(jnp.int32, (A_QK, LANES), 0)
           == lax.broadcasted_iota(jnp.int32, (A_QK, LANES), 1))
    bodies = [(d, h) for d in range(2) for h in range(HEADS)]

    def operands(d, h):
        qa, va, qb, vb, kt = dirs[d][:5]
        return dict(
            q_a=qa[:, h * A_QK:(h + 1) * A_QK],
            kt_a=kt[h * A_QK:(h + 1) * A_QK, :],
            v_a=va[:, h * HEAD_V:(h + 1) * HEAD_V],
            q_b=qb[:, h * B_QK:(h + 1) * B_QK],
            kt_b=kt[A_QK_W + h * B_QK:A_QK_W + (h + 1) * B_QK, :],
            v_b=vb[:, h * HEAD_V:(h + 1) * HEAD_V])


    prep = {}
    for d, h in bodies:
        idx = d * HEADS + h
        g = dirs[d][5]
        end = 0 if d == 1 else L - 1
        r_row = g[idx:idx + 1, :]
        cb_row = g[8 + idx:8 + idx + 1, :]
        cmr_row = g[16 + idx:16 + idx + 1, :]
        cb_col = jnp.transpose(cb_row)
        cmr_col = jnp.transpose(cmr_row)
        m_old = m_ref[idx][0:1, 0:1]
        big_m = jnp.maximum(m_old, cmr_col)
        m_top = jnp.maximum(m_old, cmr_row[:, end:end + 1])
        prep[(d, h)] = dict(
            r2_row=r_row * LOG2E, big_m2=big_m * LOG2E,
            w_inter=jnp.exp(m_old - big_m), floor=jnp.exp(-(cb_col + big_m)),
            w_s=jnp.exp(r_row - m_top), dec=jnp.exp(m_old - m_top),
            m_new=cb_row[:, end:end + 1] + m_top)

    half = L // 2
    t_h = lax.broadcasted_iota(jnp.int32, (half, half), 0)
    s_h = lax.broadcasted_iota(jnp.int32, (half, half), 1)
    st2 = {}
    for d, h in bodies:
        idx = d * HEADS + h
        o = operands(d, h)
        pr = prep[(d, h)]
        tri = (s_h >= t_h) if d == 1 else (s_h <= t_h)
        nr = slice(half, L) if d == 1 else slice(0, half)
        wr = slice(0, half) if d == 1 else slice(half, L)
        r2, bm = pr["r2_row"], pr["big_m2"]
        e_n = jnp.where(tri, jnp.exp2(r2[:, nr] - bm[nr]), 0.0)
        e_w_diag = jnp.where(tri, jnp.exp2(r2[:, wr] - bm[wr]), 0.0)
        e_w_full = jnp.exp2(r2[:, nr] - bm[wr])
        e_w = jnp.concatenate([e_w_diag, e_w_full] if d == 1 else [e_w_full, e_w_diag], axis=1)
        p_n = _dot(o["q_a"][nr], o["kt_a"][:, nr]) * e_n
        p_w = _dot(o["q_a"][wr], o["kt_a"]) * e_w
        dm = dm_ref[idx]
        a_n = _dot(o["q_b"][nr], o["kt_b"][:, nr]) * dm[nr, nr]
        a_w = _dot(o["q_b"][wr], o["kt_b"]) * dm[wr, :]
        kw_t = o["kt_a"].astype(F32) * pr["w_s"]
        kbw_t = o["kt_b"].astype(F32) * rc_ref[idx][0:1, :]
        st2[(d, h)] = dict(
            blocks=((nr, nr, p_n.astype(BF16), jnp.sum(p_n, axis=1, keepdims=True), a_n.astype(BF16)),
                    (wr, slice(0, L), p_w.astype(BF16), jnp.sum(p_w, axis=1, keepdims=True), a_w.astype(BF16))),
            kw=kw_t.astype(BF16), kwsum=jnp.sum(kw_t, axis=1, keepdims=True), kbw=kbw_t.astype(BF16))

    for d, h in bodies:
        idx = d * HEADS + h
        o = operands(d, h)
        pr, s2 = prep[(d, h)], st2[(d, h)]
        h_out, r_out = dirs[d][6], dirs[d][7]
        c_old = c_ref[idx]
        r_old = r_ref[idx]
        nv_old = nv_ref[idx]
        wi = wi_ref[idx]
        wi2 = jnp.concatenate([wi, wi], axis=1)
        nv_row = jnp.sum(jnp.where(eye, nv_old, 0.0), axis=0, keepdims=True)
        q_nv = jnp.sum(o["q_a"].astype(F32) * nv_row, axis=1, keepdims=True)
        qc = _dot(o["q_a"], c_old.astype(BF16))
        qr = _dot(o["q_b"], r_old.astype(BF16))
        for rows, cols, pb, psum, ab in s2["blocks"]:
            den = psum + pr["w_inter"][rows] * q_nv[rows]
            inv = 1.0 / jnp.maximum(jnp.abs(den), pr["floor"][rows])
            num = _dot(pb, o["v_a"][cols]) + pr["w_inter"][rows] * qc[rows]
            h_out[rows, h * HEAD_V:(h + 1) * HEAD_V] = (num * inv).astype(h_out.dtype)
            r_out[rows, h * HEAD_V:(h + 1) * HEAD_V] = (
                _dot(ab, o["v_b"][cols]) + wi2[rows] * qr[rows]).astype(r_out.dtype)
        c_ref[idx] = pr["dec"] * c_old + _dot(s2["kw"], o["v_a"])
        nv_ref[idx] = pr["dec"] * nv_old + s2["kwsum"]
        m_ref[idx] = jnp.broadcast_to(pr["m_new"], m_ref.shape[1:])
        r_ref[idx] = rc_ref[idx][1:2, 0:1] * r_old + _dot(s2["kbw"], o["v_b"])


def _mixer_call(lg, qa, va, qb, vb, kt, gt, batch, seq):
    L = MIX_L
    step = MIX_SUB * L
    n = seq // step
    qa, va, qb, vb = (a.reshape(batch, seq, a.shape[-1]) for a in (qa, va, qb, vb))

    def specs(block_of):
        tokm = lambda w: pl.BlockSpec((None, step, w), lambda b, c: (b, block_of(c), 0))
        featm = lambda r: pl.BlockSpec((None, r, step), lambda b, c: (b, 0, block_of(c)))
        return [tokm(A_QK_W), tokm(V_W), tokm(B_QK_W), tokm(V_W), featm(KT_ROWS), featm(N_GROWS)]

    fwd = lambda c: c
    bwd = lambda c: n - 1 - c
    out_f = pl.BlockSpec((None, step, V_W), lambda b, c: (b, c, 0))
    out_b = pl.BlockSpec((None, step, V_W), lambda b, c: (b, n - 1 - c, 0))
    out_sds = jax.ShapeDtypeStruct((batch, seq, V_W), MIX_OUT_DTYPE)
    nstate = 2 * HEADS
    return pl.pallas_call(
        _mixer_kernel,
        grid=(batch, n),
        in_specs=[pl.BlockSpec(memory_space=pltpu.SMEM)] + specs(fwd) + specs(bwd),
        out_specs=[out_f, out_b, out_f, out_b],
        out_shape=[out_sds] * 4,
        scratch_shapes=[pltpu.VMEM((nstate, A_QK, HEAD_V), F32),
                        pltpu.VMEM((nstate, A_QK, LANES), F32),
                        pltpu.VMEM((nstate, 8, LANES), F32),
                        pltpu.VMEM((nstate, B_QK, HEAD_V), F32),
                        pltpu.VMEM((nstate, L, L), F32),
                        pltpu.VMEM((nstate, L, LANES), F32),
                        pltpu.VMEM((nstate, 8, L), F32)],
        compiler_params=pltpu.CompilerParams(dimension_semantics=("parallel", "arbitrary"),
                                             vmem_limit_bytes=VMEM_LIMIT),
        name="mixer",
    )(lg, qa, va, qb, vb, kt, gt, qa, va, qb, vb, kt, gt)


def _head_norm(h):
    mu = jnp.mean(h, axis=-1, keepdims=True)
    hc = h - mu
    var = jnp.mean(hc * hc, axis=-1, keepdims=True)
    return hc * lax.rsqrt(var + LN_EPS)


def _sigmoid(x):
    return 0.5 * jnp.tanh(0.5 * x) + 0.5


def _merge_kernel(x_ref, hf_ref, hb_ref, rf_ref, rb_ref, w4_ref, ga_ref, gb_ref, wo_ref, lng_ref, lnb_ref,
                  o_ref, merged_ref, *, alpha):
    x = x_ref[...]
    xb = x.astype(BF16)
    for h in range(HEADS):
        sl = slice(h * HEAD_V, (h + 1) * HEAD_V)
        col = lambda piece: slice(piece * V_W + h * HEAD_V, piece * V_W + (h + 1) * HEAD_V)
        o_gate = _sigmoid(_dot(xb, w4_ref[:, col(0)]))
        both = lambda f_ref, b_ref: f_ref[:, sl].astype(F32) + b_ref[:, sl].astype(F32)
        y_a = _head_norm(both(hf_ref, hb_ref) * o_gate) * ga_ref[:, sl]
        g_pre = _dot(xb, w4_ref[:, col(1)])
        y_b = _head_norm(both(rf_ref, rb_ref)) * gb_ref[:, sl] * (g_pre * _sigmoid(g_pre))
        m_a = _sigmoid(_dot(xb, w4_ref[:, col(2)]))
        m_b = _sigmoid(_dot(xb, w4_ref[:, col(3)]))
        merged_ref[:, sl] = (m_a * y_a + m_b * y_b).astype(BF16)
    z = alpha * x + _dot(merged_ref[...], wo_ref[...])
    o_ref[...] = _layer_norm(z, lng_ref[...], lnb_ref[...])


def _merge_call(x2d, hf, hb, rf, rb, w4, g_a, g_b, wo, ln_g, ln_b, alpha):
    tokens = x2d.shape[0]
    tm = MERGE_TM
    tok = pl.BlockSpec((tm, D_MODEL), lambda i: (i, 0))
    flat = lambda a: a.reshape(tokens, V_W)
    return pl.pallas_call(
        functools.partial(_merge_kernel, alpha=alpha),
        grid=(tokens // tm,),
        in_specs=[tok, tok, tok, tok, tok, _resident(w4.shape), _resident(g_a.shape), _resident(g_b.shape),
                  _resident(wo.shape), _resident(ln_g.shape), _resident(ln_b.shape)],
        out_specs=tok,
        out_shape=jax.ShapeDtypeStruct((tokens, D_MODEL), F32),
        scratch_shapes=[pltpu.VMEM((tm, V_W), BF16)],
        compiler_params=pltpu.CompilerParams(dimension_semantics=("parallel",),
                                             vmem_limit_bytes=VMEM_LIMIT),
        name="merge",
    )(x2d, flat(hf), flat(hb), flat(rf), flat(rb), w4, g_a, g_b, wo, ln_g, ln_b)


def _ffn_kernel(x_ref, xp_ref, xn_ref, wup_ref, cw_ref, cb_ref, f_ref, hv_ref, hg_ref, *, tiles_per_seq):
    tm = FFN_TM
    rows = tm + 2 * HALO
    n_slabs = FFN_TF // LANES
    pos = lax.rem(pl.program_id(0), tiles_per_seq)
    x = x_ref[...]
    x_prev = jnp.where(pos == 0, 0.0, xp_ref[...])
    x_next = jnp.where(pos == tiles_per_seq - 1, 0.0, xn_ref[...])
    xe = jnp.concatenate([x_prev, x, x_next], axis=0).astype(BF16)

    def put(h_ref, slot, hu):
        for c in range(n_slabs):
            h_ref[slot, c, pl.ds(0, rows, stride=2), :] = hu[:, c * LANES:(c + 1) * LANES]

    def up(j):
        c_val = j * FFN_TF
        c_gate = D_FF + j * FFN_TF
        put(hv_ref, j % FFN_SLOTS, _dot(xe, wup_ref[:, c_val:c_val + FFN_TF]))
        put(hg_ref, j % FFN_SLOTS, _dot(xe, wup_ref[:, c_gate:c_gate + FFN_TF]))

    def conv(h_ref, slot, c0):
        out = []
        for c in range(n_slabs):
            cols = slice(c0 + c * LANES, c0 + (c + 1) * LANES)
            w = cw_ref[:, cols]
            taps = [h_ref[slot, c, pl.ds(2 * (HALO - 1 + k), tm, stride=2), :] for k in range(3)]
            out.append(w[0:1] * taps[0] + w[1:2] * taps[1] + w[2:3] * taps[2] + cb_ref[:, cols])
        return jnp.concatenate(out, axis=1)

    n_chunks = D_FF // FFN_TF
    for j in range(FFN_SLOTS - 1):
        up(j)
    for j in range(n_chunks):
        if j + FFN_SLOTS - 1 < n_chunks:
            up(j + FFN_SLOTS - 1)
        c_val = j * FFN_TF
        slot = j % FFN_SLOTS
        f_ref[:, c_val:c_val + FFN_TF] = (
            jax.nn.gelu(conv(hg_ref, slot, D_FF + c_val)) * conv(hv_ref, slot, c_val)).astype(BF16)


def _ffn_down_kernel(x_ref, f_ref, wdn_ref, lng_ref, lnb_ref, o_ref, *, alpha):
    o_ref[...] = _layer_norm(alpha * x_ref[...] + _dot(f_ref[...], wdn_ref[...]), lng_ref[...], lnb_ref[...])


def _ffn_call(x2d, seq, wup, cw, cb, wdn, ln_g, ln_b, alpha):
    tokens = x2d.shape[0]
    tm = FFN_TM
    per = tm // HALO
    last = tokens // HALO - 1
    tok = pl.BlockSpec((tm, D_MODEL), lambda i: (i, 0))
    slot_shape = (FFN_SLOTS, FFN_TF // LANES, 2 * (tm + 2 * HALO), LANES)
    ftok = pl.BlockSpec((tm, D_FF), lambda i: (i, 0))
    f = pl.pallas_call(
        functools.partial(_ffn_kernel, tiles_per_seq=seq // tm),
        grid=(tokens // tm,),
        in_specs=[tok,
                  pl.BlockSpec((HALO, D_MODEL), lambda i: (jnp.maximum(i * per - 1, 0), 0)),
                  pl.BlockSpec((HALO, D_MODEL), lambda i: (jnp.minimum((i + 1) * per, last), 0)),
                  _resident(wup.shape), _resident(cw.shape), _resident(cb.shape)],
        out_specs=ftok,
        out_shape=jax.ShapeDtypeStruct((tokens, D_FF), BF16),
        scratch_shapes=[pltpu.VMEM(slot_shape, F32),
                        pltpu.VMEM(slot_shape, F32)],
        compiler_params=pltpu.CompilerParams(dimension_semantics=("parallel",),
                                             vmem_limit_bytes=VMEM_LIMIT),
        name="ffn_up",
    )(x2d, x2d, x2d, wup, cw, cb)
    return pl.pallas_call(
        functools.partial(_ffn_down_kernel, alpha=alpha),
        grid=(tokens // tm,),
        in_specs=[tok, ftok, _resident(wdn.shape), _resident(ln_g.shape), _resident(ln_b.shape)],
        out_specs=tok,
        out_shape=jax.ShapeDtypeStruct((tokens, D_MODEL), F32),
        compiler_params=pltpu.CompilerParams(dimension_semantics=("parallel",),
                                             vmem_limit_bytes=VMEM_LIMIT),
        name="ffn_down",
    )(x2d, f, wdn, ln_g, ln_b)


def _rope_tables(seq):
    inv_freq = ROPE_BASE ** (-jnp.arange(0, B_QK, 2, dtype=F32) / B_QK)
    ang = jnp.arange(seq, dtype=F32)[:, None] * inv_freq[None, :]
    cos = jnp.cos(ang)
    sin = jnp.sin(ang)
    return cos, sin, cos.T, sin.T


def _layer_weights(l, w_in, b_if, w_o, w_up, w_down):
    w = w_in[l]
    cols = lambda o, n: w[:, o:o + n]
    wtok = jnp.concatenate([cols(O_QA, A_QK_W), cols(O_VA, V_W), cols(O_QB, B_QK_W), cols(O_VB, V_W)],
                           axis=1).astype(BF16)
    wt = jnp.concatenate([cols(O_KA, A_QK_W), cols(O_KB, B_QK_W)], axis=1).T.astype(BF16)
    wg = cols(O_G, N_GATES).T.astype(BF16)
    bg = b_if[l].astype(F32).reshape(N_GATES, 1)
    w4 = jnp.concatenate([cols(O_OA, V_W), cols(O_GB, V_W), cols(O_MA, D_MODEL), cols(O_MB, D_MODEL)],
                         axis=1).astype(BF16)
    return wtok, wt, wg, bg, w4, w_o[l].astype(BF16), w_up[l].astype(BF16), w_down[l].astype(BF16)


def _trunk(x, ln_in, layers, alpha):
    batch, seq, _ = x.shape
    assert seq % PROJ_TM == 0 and seq % FFN_TM == 0 and seq % (MIX_SUB * MIX_L) == 0
    rope = _rope_tables(seq)
    x2d = x.reshape(batch * seq, D_MODEL)
    for l, lw in enumerate(layers):
        (wtok, wt, wg, bg, w4, wo, wup, wdn, lg, g_a, g_b, ln1, cw, cb, ln2) = lw
        outs = _proj_call(x2d, batch, seq, ln_in if l == 0 else None, wtok, wt, wg, bg, rope)
        qa, va, qb, vb, kt, gt = outs[:6]
        if l == 0:
            x2d = outs[6]
        hf, hb, rf, rb = _mixer_call(lg, qa, va, qb, vb, kt, _gates_call(gt), batch, seq)
        x2d = _merge_call(x2d, hf, hb, rf, rb, w4, g_a, g_b, wo, ln1[0], ln1[1], alpha)
        x2d = _ffn_call(x2d, seq, wup, cw, cb, wdn, ln2[0], ln2[1], alpha)
    return x2d.reshape(batch, seq, D_MODEL)


def kernel(x_prompt, x_sample, ln_in_g, ln_in_b, w_in, b_if, ret_log_decay, mlstm_norm_g, ret_norm_g, w_o, ln1_g, ln1_b, w_up, conv_w, conv_b, w_down, ln2_g, ln2_b):
    depth = w_in.shape[0]
    alpha = (2 * depth) ** 0.25
    row = lambda a: a.astype(F32).reshape(1, -1)
    layers = []
    for l in range(depth):
        layers.append(_layer_weights(l, w_in, b_if, w_o, w_up, w_down) + (
            ret_log_decay[l].astype(F32), row(mlstm_norm_g[l]), row(ret_norm_g[l]),
            (row(ln1_g[l]), row(ln1_b[l])), conv_w[l].astype(F32), row(conv_b[l]),
            (row(ln2_g[l]), row(ln2_b[l]))))
    ln_in = (row(ln_in_g), row(ln_in_b))
    return (_trunk(x_prompt, ln_in, layers, alpha), _trunk(x_sample, ln_in, layers, alpha))
```

```python
import functools

import jax
import jax.numpy as jnp
from jax import lax
from jax.experimental import pallas as pl
from jax.experimental.pallas import tpu as pltpu

F32 = jnp.float32
BF16 = jnp.bfloat16

D_MODEL = 1024
HEADS = 4
A_QK = 128
HEAD_V = 256
B_QK = 256
D_FF = 2816
MIX_L = 256
MIX_SUB = 2
LANES = 128
LOG2E = 1.4426950408889634
LN_EPS = 1e-5
ROPE_BASE = 10000.0

A_QK_W = HEADS * A_QK
V_W = HEADS * HEAD_V
B_QK_W = HEADS * B_QK
N_GATES = 16
N_GROWS = 24
KT_ROWS = A_QK_W + B_QK_W

_SPLITS = (A_QK_W, A_QK_W, V_W, V_W, N_GATES, B_QK_W, B_QK_W, V_W, V_W, D_MODEL, D_MODEL)
_OFF = [0]
for _w in _SPLITS:
    _OFF.append(_OFF[-1] + _w)
(O_QA, O_KA, O_VA, O_OA, O_G, O_QB, O_KB, O_VB, O_GB, O_MA, O_MB, _) = _OFF

PROJ_TM = 1024
MERGE_TM = 256
MIX_OUT_DTYPE = jnp.float32
FFN_TM = 512
FFN_TF = 256
FFN_SLOTS = 4
HALO = 8
VMEM_LIMIT = 56 * 1024 * 1024

_NT = (((1,), (1,)), ((), ()))


def _dot(a, b):
    return jnp.dot(a, b, preferred_element_type=F32)


def _dot_nt(a, b):
    return lax.dot_general(a, b, _NT, preferred_element_type=F32)


def _layer_norm(x, g, b):
    mu = jnp.mean(x, axis=-1, keepdims=True)
    xc = x - mu
    var = jnp.mean(xc * xc, axis=-1, keepdims=True)
    return xc * lax.rsqrt(var + LN_EPS) * g + b


def _resident(shape):
    nd = len(shape)
    return pl.BlockSpec(shape, lambda *_: (0,) * nd, pipeline_mode=pl.Buffered(1))


def _scan_chunks(x, op, identity, reverse):
    per_chunk = MIX_L // LANES
    blocks = [x[:, j * LANES:(j + 1) * LANES] for j in range(x.shape[1] // LANES)]
    lane = lax.broadcasted_iota(jnp.int32, blocks[0].shape, 1)
    for j, y in enumerate(blocks):
        k = 1
        while k < LANES:
            if reverse:
                y = op(y, jnp.where(lane < LANES - k, pltpu.roll(y, LANES - k, axis=1), identity))
            else:
                y = op(y, jnp.where(lane >= k, pltpu.roll(y, k, axis=1), identity))
            k *= 2
        blocks[j] = y
    for c0 in range(0, len(blocks), per_chunk):
        if reverse:
            for j in range(c0 + per_chunk - 2, c0 - 1, -1):
                blocks[j] = op(blocks[j], blocks[j + 1][:, 0:1])
        else:
            for j in range(c0 + 1, c0 + per_chunk):
                blocks[j] = op(blocks[j], blocks[j - 1][:, LANES - 1:LANES])
    return jnp.concatenate(blocks, axis=1)


def _proj_kernel(*refs, apply_ln):
    if apply_ln:
        (x_ref, lng_ref, lnb_ref, wtok_ref, wt_ref, wg_ref, bg_ref, cos_ref, sin_ref, cost_ref, sint_ref,
         qa_ref, va_ref, qb_ref, vb_ref, kt_ref, gt_ref, xn_ref) = refs
        x = _layer_norm(x_ref[...], lng_ref[...], lnb_ref[...])
        xn_ref[...] = x
    else:
        (x_ref, wtok_ref, wt_ref, wg_ref, bg_ref, cos_ref, sin_ref, cost_ref, sint_ref,
         qa_ref, va_ref, qb_ref, vb_ref, kt_ref, gt_ref) = refs
        x = x_ref[...]
    xb = x.astype(BF16)

    qa_ref[...] = (_dot(xb, wtok_ref[:, 0:A_QK_W]) * (A_QK ** -0.5)).astype(BF16)
    half = V_W // 2
    for j in range(2):
        c0 = A_QK_W + j * half
        va_ref[:, j * half:(j + 1) * half] = _dot(xb, wtok_ref[:, c0:c0 + half]).astype(BF16)
    cos = cos_ref[...]
    sin = sin_ref[...]
    hq = B_QK // 2
    for h in range(HEADS):
        c0 = A_QK_W + V_W + h * B_QK
        t = _dot(xb, wtok_ref[:, c0:c0 + B_QK])
        t1 = t[:, :hq]
        t2 = t[:, hq:]
        qb_ref[:, h * B_QK:h * B_QK + hq] = (t1 * cos - t2 * sin).astype(BF16)
        qb_ref[:, h * B_QK + hq:(h + 1) * B_QK] = (t1 * sin + t2 * cos).astype(BF16)
    for j in range(2):
        c0 = A_QK_W + V_W + B_QK_W + j * half
        vb_ref[:, j * half:(j + 1) * half] = _dot(xb, wtok_ref[:, c0:c0 + half]).astype(BF16)

    for j in range(2):
        r0 = j * 256
        kt_ref[r0:r0 + 256, :] = _dot_nt(wt_ref[r0:r0 + 256, :], xb).astype(BF16)
    cost = cost_ref[...]
    sint = sint_ref[...]
    for h in range(HEADS):
        r0 = A_QK_W + h * B_QK
        t = _dot_nt(wt_ref[r0:r0 + B_QK, :], xb)
        t1 = t[:hq, :]
        t2 = t[hq:, :]
        kt_ref[r0:r0 + hq, :] = ((t1 * cost - t2 * sint) * (B_QK ** -0.5)).astype(BF16)
        kt_ref[r0 + hq:r0 + B_QK, :] = ((t1 * sint + t2 * cost) * (B_QK ** -0.5)).astype(BF16)

    g = _dot_nt(wg_ref[...], xb) + bg_ref[...]
    row = lax.broadcasted_iota(jnp.int32, g.shape, 0)
    log_sig = jnp.minimum(g, 0.0) - jnp.log1p(jnp.exp(-jnp.abs(g)))
    gt_ref[...] = jnp.where((row & HEADS) != 0, log_sig, g)


def _gates_kernel(g_ref, o_ref):
    g = g_ref[...]
    first4 = lax.broadcasted_iota(jnp.int32, (8, g.shape[1]), 0) < HEADS
    r_pair, cb_pair, cmr_pair = [], [], []
    for d in range(2):
        gd = g[d * 8:(d + 1) * 8, :]
        cums = _scan_chunks(gd, jnp.add, 0.0, reverse=(d == 1))
        r = gd - pltpu.roll(cums, HEADS, axis=0)
        r_pair.append(r)
        cb_pair.append(cums)
        cmr_pair.append(_scan_chunks(r, jnp.maximum, -jnp.inf, reverse=(d == 1)))
    o_ref[0:8, :] = jnp.where(first4, r_pair[0], pltpu.roll(r_pair[1], HEADS, axis=0))
    o_ref[8:16, :] = jnp.where(first4, pltpu.roll(cb_pair[0], HEADS, axis=0), cb_pair[1])
    o_ref[16:24, :] = jnp.where(first4, cmr_pair[0], pltpu.roll(cmr_pair[1], HEADS, axis=0))


def _gates_call(gt):
    batch, _, seq = gt.shape
    return pl.pallas_call(
        _gates_kernel,
        grid=(batch,),
        in_specs=[pl.BlockSpec((None, N_GATES, seq), lambda b: (b, 0, 0))],
        out_specs=pl.BlockSpec((None, N_GROWS, seq), lambda b: (b, 0, 0)),
        out_shape=jax.ShapeDtypeStruct((batch, N_GROWS, seq), F32),
        compiler_params=pltpu.CompilerParams(dimension_semantics=("parallel",),
                                             vmem_limit_bytes=VMEM_LIMIT),
        name="gates",
    )(gt)


def _proj_call(x2d, batch, seq, ln, wtok, wt, wg, bg, rope):
    tokens = x2d.shape[0]
    tm = PROJ_TM
    nt = seq // tm
    cos, sin, cost, sint = rope
    tok = lambda w: pl.BlockSpec((tm, w), lambda i: (i, 0))
    in_specs = [tok(D_MODEL)]
    args = [x2d]
    if ln is not None:
        in_specs += [_resident((1, D_MODEL)), _resident((1, D_MODEL))]
        args += list(ln)
    in_specs += [_resident(wtok.shape), _resident(wt.shape), _resident(wg.shape), _resident(bg.shape),
                 pl.BlockSpec((tm, B_QK // 2), lambda i: (i % nt, 0)),
                 pl.BlockSpec((tm, B_QK // 2), lambda i: (i % nt, 0)),
                 pl.BlockSpec((B_QK // 2, tm), lambda i: (0, i % nt)),
                 pl.BlockSpec((B_QK // 2, tm), lambda i: (0, i % nt))]
    args += [wtok, wt, wg, bg, cos, sin, cost, sint]
    out_shape = [jax.ShapeDtypeStruct((tokens, A_QK_W), BF16),
                 jax.ShapeDtypeStruct((tokens, V_W), BF16),
                 jax.ShapeDtypeStruct((tokens, B_QK_W), BF16),
                 jax.ShapeDtypeStruct((tokens, V_W), BF16),
                 jax.ShapeDtypeStruct((batch, KT_ROWS, seq), BF16),
                 jax.ShapeDtypeStruct((batch, N_GATES, seq), F32)]
    out_specs = [tok(A_QK_W), tok(V_W), tok(B_QK_W), tok(V_W),
                 pl.BlockSpec((None, KT_ROWS, tm), lambda i: (i // nt, 0, i % nt)),
                 pl.BlockSpec((None, N_GATES, tm), lambda i: (i // nt, 0, i % nt))]
    if ln is not None:
        out_shape.append(jax.ShapeDtypeStruct((tokens, D_MODEL), F32))
        out_specs.append(tok(D_MODEL))
    return pl.pallas_call(
        functools.partial(_proj_kernel, apply_ln=ln is not None),
        grid=(tokens // tm,),
        in_specs=in_specs,
        out_specs=out_specs,
        out_shape=out_shape,
        compiler_params=pltpu.CompilerParams(dimension_semantics=("parallel",),
                                             vmem_limit_bytes=VMEM_LIMIT),
        name="proj",
    )(*args)


def _mixer_kernel(lg_ref,
                  qa_f, va_f, qb_f, vb_f, kt_f, g_f,
                  qa_b, va_b, qb_b, vb_b, kt_b, g_b,
                  hf_ref, hb_ref, rf_ref, rb_ref,
                  c_ref, nv_ref, m_ref, r_ref, dm_ref, wi_ref, rc_ref):
    L = MIX_L
    t_i = lax.broadcasted_iota(jnp.int32, (L, L), 0)
    s_i = lax.broadcasted_iota(jnp.int32, (L, L), 1)

    @pl.when(pl.program_id(1) == 0)
    def _init():
        c_ref[...] = jnp.zeros_like(c_ref)
        nv_ref[...] = jnp.zeros_like(nv_ref)
        m_ref[...] = jnp.zeros_like(m_ref)
        r_ref[...] = jnp.zeros_like(r_ref)
        tcol = lax.broadcasted_iota(jnp.int32, (L, LANES), 0).astype(F32)
        srow = lax.broadcasted_iota(jnp.int32, (8, L), 1).astype(F32)
        row8 = lax.broadcasted_iota(jnp.int32, (8, L), 0)
        for d in range(2):
            diff = ((t_i - s_i) if d == 0 else (s_i - t_i)).astype(F32)
            for h in range(HEADS):
                lg = lg_ref[d, h]
                dm_ref[d * HEADS + h] = jnp.where(diff >= 0.0, jnp.exp(lg * jnp.maximum(diff, 0.0)), 0.0)
                if d == 0:
                    wi = jnp.exp(lg * (tcol + 1.0))
                    ws = jnp.exp(lg * (L - 1.0 - srow))
                else:
                    wi = jnp.exp(lg * (L - tcol))
                    ws = jnp.exp(lg * srow)
                wi_ref[d * HEADS + h] = wi
                chunk_dec = jnp.exp(jnp.full((8, L), lg * float(L), F32))
                rc_ref[d * HEADS + h] = jnp.where(row8 == 0, ws, chunk_dec)

    for sub in range(MIX_SUB):
        views = []
        for off, refs in ((sub * L, (qa_f, va_f, qb_f, vb_f, kt_f, g_f, hf_ref, rf_ref)),
                          ((MIX_SUB - 1 - sub) * L, (qa_b, va_b, qb_b, vb_b, kt_b, g_b, hb_ref, rb_ref))):
            qa, va, qb, vb, kt, g, h_out, r_out = refs
            tok = lambda ref: ref.at[off:off + L, :]
            views.append((tok(qa), tok(va), tok(qb), tok(vb), kt.at[:, off:off + L], g.at[:, off:off + L],
                          tok(h_out), tok(r_out)))
        _mixer_chunk(tuple(views), c_ref, nv_ref, m_ref, r_ref, dm_ref, wi_ref, rc_ref)


def _mixer_chunk(dirs, c_ref, nv_ref, m_ref, r_ref, dm_ref, wi_ref, rc_ref):
    L = MIX_L
    eye = (lax.broadcasted_iota(jnp.int32, (A_QK, LANES), 0)
           == lax.broadcasted_iota(jnp.int32, (A_QK, LANES), 1))
    bodies = [(d, h) for d in range(2) for h in range(HEADS)]

    def operands(d, h):
        qa, va, qb, vb, kt = dirs[d][:5]
        return dict(
            q_a=qa[:, h * A_QK:(h + 1) * A_QK],
            kt_a=kt[h * A_QK:(h + 1) * A_QK, :],
            v_a=va[:, h * HEAD_V:(h + 1) * HEAD_V],
            q_b=qb[:, h * B_QK:(h + 1) * B_QK],
            kt_b=kt[A_QK_W + h * B_QK:A_QK_W + (h + 1) * B_QK, :],
            v_b=vb[:, h * HEAD_V:(h + 1) * HEAD_V])


    prep = {}
    for d, h in bodies:
        idx = d * HEADS + h
        g = dirs[d][5]
        end = 0 if d == 1 else L - 1
        r_row = g[idx:idx + 1, :]
        cb_row = g[8 + idx:8 + idx + 1, :]
        cmr_row = g[16 + idx:16 + idx + 1, :]
        cb_col = jnp.transpose(cb_row)
        cmr_col = jnp.transpose(cmr_row)
        m_old = m_ref[idx][0:1, 0:1]
        big_m = jnp.maximum(m_old, cmr_col)
        m_top = jnp.maximum(m_old, cmr_row[:, end:end + 1])
        prep[(d, h)] = dict(
            r2_row=r_row * LOG2E, big_m2=big_m * LOG2E,
            w_inter=jnp.exp(m_old - big_m), floor=jnp.exp(-(cb_col + big_m)),
            w_s=jnp.exp(r_row - m_top), dec=jnp.exp(m_old - m_top),
            m_new=cb_row[:, end:end + 1] + m_top)

    half = L // 2
    t_h = lax.broadcasted_iota(jnp.int32, (half, half), 0)
    s_h = lax.broadcasted_iota(jnp.int32, (half, half), 1)
    st2 = {}
    for d, h in bodies:
        idx = d * HEADS + h
        o = operands(d, h)
        pr = prep[(d, h)]
        tri = (s_h >= t_h) if d == 1 else (s_h <= t_h)
        nr = slice(half, L) if d == 1 else slice(0, half)
        wr = slice(0, half) if d == 1 else slice(half, L)
        r2, bm = pr["r2_row"], pr["big_m2"]
        e_n = jnp.where(tri, jnp.exp2(r2[:, nr] - bm[nr]), 0.0)
        e_w_diag = jnp.where(tri, jnp.exp2(r2[:, wr] - bm[wr]), 0.0)
        e_w_full = jnp.exp2(r2[:, nr] - bm[wr])
        e_w = jnp.concatenate([e_w_diag, e_w_full] if d == 1 else [e_w_full, e_w_diag], axis=1)
        p_n = _dot(o["q_a"][nr], o["kt_a"][:, nr]) * e_n
        p_w = _dot(o["q_a"][wr], o["kt_a"]) * e_w
        dm = dm_ref[idx]
        a_n = _dot(o["q_b"][nr], o["kt_b"][:, nr]) * dm[nr, nr]
        a_w = _dot(o["q_b"][wr], o["kt_b"]) * dm[wr, :]
        kw_t = o["kt_a"].astype(F32) * pr["w_s"]
        kbw_t = o["kt_b"].astype(F32) * rc_ref[idx][0:1, :]
        st2[(d, h)] = dict(
            blocks=((nr, nr, p_n.astype(BF16), jnp.sum(p_n, axis=1, keepdims=True), a_n.astype(BF16)),
                    (wr, slice(0, L), p_w.astype(BF16), jnp.sum(p_w, axis=1, keepdims=True), a_w.astype(BF16))),
            kw=kw_t.astype(BF16), kwsum=jnp.sum(kw_t, axis=1, keepdims=True), kbw=kbw_t.astype(BF16))

    for d, h in bodies:
        idx = d * HEADS + h
        o = operands(d, h)
        pr, s2 = prep[(d, h)], st2[(d, h)]
        h_out, r_out = dirs[d][6], dirs[d][7]
        c_old = c_ref[idx]
        r_old = r_ref[idx]
        nv_old = nv_ref[idx]
        wi = wi_ref[idx]
        wi2 = jnp.concatenate([wi, wi], axis=1)
        nv_row = jnp.sum(jnp.where(eye, nv_old, 0.0), axis=0, keepdims=True)
        q_nv = jnp.sum(o["q_a"].astype(F32) * nv_row, axis=1, keepdims=True)
        qc = _dot(o["q_a"], c_old.astype(BF16))
        qr = _dot(o["q_b"], r_old.astype(BF16))
        for rows, cols, pb, psum, ab in s2["blocks"]:
            den = psum + pr["w_inter"][rows] * q_nv[rows]
            inv = 1.0 / jnp.maximum(jnp.abs(den), pr["floor"][rows])
            num = _dot(pb, o["v_a"][cols]) + pr["w_inter"][rows] * qc[rows]
            h_out[rows, h * HEAD_V:(h + 1) * HEAD_V] = (num * inv).astype(h_out.dtype)
            r_out[rows, h * HEAD_V:(h + 1) * HEAD_V] = (
                _dot(ab, o["v_b"][cols]) + wi2[rows] * qr[rows]).astype(r_out.dtype)
        c_ref[idx] = pr["dec"] * c_old + _dot(s2["kw"], o["v_a"])
        nv_ref[idx] = pr["dec"] * nv_old + s2["kwsum"]
        m_ref[idx] = jnp.broadcast_to(pr["m_new"], m_ref.shape[1:])
        r_ref[idx] = rc_ref[idx][1:2, 0:1] * r_old + _dot(s2["kbw"], o["v_b"])


def _mixer_call(lg, qa, va, qb, vb, kt, gt, batch, seq):
    L = MIX_L
    step = MIX_SUB * L
    n = seq // step
    qa, va, qb, vb = (a.reshape(batch, seq, a.shape[-1]) for a in (qa, va, qb, vb))

    def specs(block_of):
        tokm = lambda w: pl.BlockSpec((None, step, w), lambda b, c: (b, block_of(c), 0))
        featm = lambda r: pl.BlockSpec((None, r, step), lambda b, c: (b, 0, block_of(c)))
        return [tokm(A_QK_W), tokm(V_W), tokm(B_QK_W), tokm(V_W), featm(KT_ROWS), featm(N_GROWS)]

    fwd = lambda c: c
    bwd = lambda c: n - 1 - c
    out_f = pl.BlockSpec((None, step, V_W), lambda b, c: (b, c, 0))
    out_b = pl.BlockSpec((None, step, V_W), lambda b, c: (b, n - 1 - c, 0))
    out_sds = jax.ShapeDtypeStruct((batch, seq, V_W), MIX_OUT_DTYPE)
    nstate = 2 * HEADS
    return pl.pallas_call(
        _mixer_kernel,
        grid=(batch, n),
        in_specs=[pl.BlockSpec(memory_space=pltpu.SMEM)] + specs(fwd) + specs(bwd),
        out_specs=[out_f, out_b, out_f, out_b],
        out_shape=[out_sds] * 4,
        scratch_shapes=[pltpu.VMEM((nstate, A_QK, HEAD_V), F32),
                        pltpu.VMEM((nstate, A_QK, LANES), F32),
                        pltpu.VMEM((nstate, 8, LANES), F32),
                        pltpu.VMEM((nstate, B_QK, HEAD_V), F32),
                        pltpu.VMEM((nstate, L, L), F32),
                        pltpu.VMEM((nstate, L, LANES), F32),
                        pltpu.VMEM((nstate, 8, L), F32)],
        compiler_params=pltpu.CompilerParams(dimension_semantics=("parallel", "arbitrary"),
                                             vmem_limit_bytes=VMEM_LIMIT),
        name="mixer",
    )(lg, qa, va, qb, vb, kt, gt, qa, va, qb, vb, kt, gt)


def _head_norm(h):
    mu = jnp.mean(h, axis=-1, keepdims=True)
    hc = h - mu
    var = jnp.mean(hc * hc, axis=-1, keepdims=True)
    return hc * lax.rsqrt(var + LN_EPS)


def _sigmoid(x):
    return 0.5 * jnp.tanh(0.5 * x) + 0.5


def _merge_kernel(x_ref, hf_ref, hb_ref, rf_ref, rb_ref, w4_ref, ga_ref, gb_ref, wo_ref, lng_ref, lnb_ref,
                  o_ref, merged_ref, *, alpha):
    x = x_ref[...]
    xb = x.astype(BF16)
    for h in range(HEADS):
        sl = slice(h * HEAD_V, (h + 1) * HEAD_V)
        col = lambda piece: slice(piece * V_W + h * HEAD_V, piece * V_W + (h + 1) * HEAD_V)
        o_gate = _sigmoid(_dot(xb, w4_ref[:, col(0)]))
        both = lambda f_ref, b_ref: f_ref[:, sl].astype(F32) + b_ref[:, sl].astype(F32)
        y_a = _head_norm(both(hf_ref, hb_ref) * o_gate) * ga_ref[:, sl]
        g_pre = _dot(xb, w4_ref[:, col(1)])
        y_b = _head_norm(both(rf_ref, rb_ref)) * gb_ref[:, sl] * (g_pre * _sigmoid(g_pre))
        m_a = _sigmoid(_dot(xb, w4_ref[:, col(2)]))
        m_b = _sigmoid(_dot(xb, w4_ref[:, col(3)]))
        merged_ref[:, sl] = (m_a * y_a + m_b * y_b).astype(BF16)
    z = alpha * x + _dot(merged_ref[...], wo_ref[...])
    o_ref[...] = _layer_norm(z, lng_ref[...], lnb_ref[...])


def _merge_call(x2d, hf, hb, rf, rb, w4, g_a, g_b, wo, ln_g, ln_b, alpha):
    tokens = x2d.shape[0]
    tm = MERGE_TM
    tok = pl.BlockSpec((tm, D_MODEL), lambda i: (i, 0))
    flat = lambda a: a.reshape(tokens, V_W)
    return pl.pallas_call(
        functools.partial(_merge_kernel, alpha=alpha),
        grid=(tokens // tm,),
        in_specs=[tok, tok, tok, tok, tok, _resident(w4.shape), _resident(g_a.shape), _resident(g_b.shape),
                  _resident(wo.shape), _resident(ln_g.shape), _resident(ln_b.shape)],
        out_specs=tok,
        out_shape=jax.ShapeDtypeStruct((tokens, D_MODEL), F32),
        scratch_shapes=[pltpu.VMEM((tm, V_W), BF16)],
        compiler_params=pltpu.CompilerParams(dimension_semantics=("parallel",),
                                             vmem_limit_bytes=VMEM_LIMIT),
        name="merge",
    )(x2d, flat(hf), flat(hb), flat(rf), flat(rb), w4, g_a, g_b, wo, ln_g, ln_b)


def _ffn_kernel(x_ref, xp_ref, xn_ref, wup_ref, cw_ref, cb_ref, wdn_ref, lng_ref, lnb_ref, o_ref,
                acc_ref, hv_ref, hg_ref, *, alpha, tiles_per_seq):
    tm = FFN_TM
    rows = tm + 2 * HALO
    n_slabs = FFN_TF // LANES
    pos = lax.rem(pl.program_id(0), tiles_per_seq)
    x = x_ref[...]
    x_prev = jnp.where(pos == 0, 0.0, xp_ref[...])
    x_next = jnp.where(pos == tiles_per_seq - 1, 0.0, xn_ref[...])
    xe = jnp.concatenate([x_prev, x, x_next], axis=0).astype(BF16)

    def put(h_ref, slot, hu):
        for c in range(n_slabs):
            h_ref[slot, c, pl.ds(0, rows, stride=2), :] = hu[:, c * LANES:(c + 1) * LANES]

    def up(j):
        c_val = j * FFN_TF
        c_gate = D_FF + j * FFN_TF
        put(hv_ref, j % FFN_SLOTS, _dot(xe, wup_ref[:, c_val:c_val + FFN_TF]))
        put(hg_ref, j % FFN_SLOTS, _dot(xe, wup_ref[:, c_gate:c_gate + FFN_TF]))

    def conv(h_ref, slot, c0):
        out = []
        for c in range(n_slabs):
            cols = slice(c0 + c * LANES, c0 + (c + 1) * LANES)
            w = cw_ref[:, cols]
            taps = [h_ref[slot, c, pl.ds(2 * (HALO - 1 + k), tm, stride=2), :] for k in range(3)]
            out.append(w[0:1] * taps[0] + w[1:2] * taps[1] + w[2:3] * taps[2] + cb_ref[:, cols])
        return jnp.concatenate(out, axis=1)

    n_chunks = D_FF // FFN_TF
    for j in range(FFN_SLOTS - 1):
        up(j)
    for j in range(n_chunks):
        if j + FFN_SLOTS - 1 < n_chunks:
            up(j + FFN_SLOTS - 1)
        c_val = j * FFN_TF
        slot = j % FFN_SLOTS
        f = (jax.nn.gelu(conv(hg_ref, slot, D_FF + c_val)) * conv(hv_ref, slot, c_val)).astype(BF16)
        part = _dot(f, wdn_ref[c_val:c_val + FFN_TF, :])
        if j == 0:
            acc_ref[...] = part
        else:
            acc_ref[...] += part
    o_ref[...] = _layer_norm(alpha * x + acc_ref[...], lng_ref[...], lnb_ref[...])


def _ffn_call(x2d, seq, wup, cw, cb, wdn, ln_g, ln_b, alpha):
    tokens = x2d.shape[0]
    tm = FFN_TM
    per = tm // HALO
    last = tokens // HALO - 1
    tok = pl.BlockSpec((tm, D_MODEL), lambda i: (i, 0))
    slot_shape = (FFN_SLOTS, FFN_TF // LANES, 2 * (tm + 2 * HALO), LANES)
    return pl.pallas_call(
        functools.partial(_ffn_kernel, alpha=alpha, tiles_per_seq=seq // tm),
        grid=(tokens // tm,),
        in_specs=[tok,
                  pl.BlockSpec((HALO, D_MODEL), lambda i: (jnp.maximum(i * per - 1, 0), 0)),
                  pl.BlockSpec((HALO, D_MODEL), lambda i: (jnp.minimum((i + 1) * per, last), 0)),
                  _resident(wup.shape), _resident(cw.shape), _resident(cb.shape), _resident(wdn.shape),
                  _resident(ln_g.shape), _resident(ln_b.shape)],
        out_specs=tok,
        out_shape=jax.ShapeDtypeStruct((tokens, D_MODEL), F32),
        scratch_shapes=[pltpu.VMEM((tm, D_MODEL), F32),
                        pltpu.VMEM(slot_shape, F32),
                        pltpu.VMEM(slot_shape, F32)],
        compiler_params=pltpu.CompilerParams(dimension_semantics=("parallel",),
                                             vmem_limit_bytes=VMEM_LIMIT),
        name="ffn",
    )(x2d, x2d, x2d, wup, cw, cb, wdn, ln_g, ln_b)


def _rope_tables(seq):
    inv_freq = ROPE_BASE ** (-jnp.arange(0, B_QK, 2, dtype=F32) / B_QK)
    ang = jnp.arange(seq, dtype=F32)[:, None] * inv_freq[None, :]
    cos = jnp.cos(ang)
    sin = jnp.sin(ang)
    return cos, sin, cos.T, sin.T


def _layer_weights(l, w_in, b_if, w_o, w_up, w_down):
    w = w_in[l].astype(BF16)
    cols = lambda o, n: w[:, o:o + n]
    wtok = jnp.concatenate([cols(O_QA, A_QK_W), cols(O_VA, V_W), cols(O_QB, B_QK_W), cols(O_VB, V_W)], axis=1)
    wt = jnp.concatenate([cols(O_KA, A_QK_W), cols(O_KB, B_QK_W)], axis=1).T
    wg = cols(O_G, N_GATES).T
    bg = b_if[l].astype(F32).reshape(N_GATES, 1)
    w4 = jnp.concatenate([cols(O_OA, V_W), cols(O_GB, V_W), cols(O_MA, D_MODEL), cols(O_MB, D_MODEL)], axis=1)
    return wtok, wt, wg, bg, w4, w_o[l].astype(BF16), w_up[l].astype(BF16), w_down[l].astype(BF16)


def _trunk(x, ln_in, layers, alpha):
    batch, seq, _ = x.shape
    assert seq % PROJ_TM == 0 and seq % FFN_TM == 0 and seq % (MIX_SUB * MIX_L) == 0
    rope = _rope_tables(seq)
    x2d = x.reshape(batch * seq, D_MODEL)
    for l, lw in enumerate(layers):
        (wtok, wt, wg, bg, w4, wo, wup, wdn, lg, g_a, g_b, ln1, cw, cb, ln2) = lw
        outs = _proj_call(x2d, batch, seq, ln_in if l == 0 else None, wtok, wt, wg, bg, rope)
        qa, va, qb, vb, kt, gt = outs[:6]
        if l == 0:
            x2d = outs[6]
        hf, hb, rf, rb = _mixer_call(lg, qa, va, qb, vb, kt, _gates_call(gt), batch, seq)
        x2d = _merge_call(x2d, hf, hb, rf, rb, w4, g_a, g_b, wo, ln1[0], ln1[1], alpha)
        x2d = _ffn_call(x2d, seq, wup, cw, cb, wdn, ln2[0], ln2[1], alpha)
    return x2d.reshape(batch, seq, D_MODEL)


def kernel(x_prompt, x_sample, ln_in_g, ln_in_b, w_in, b_if, ret_log_decay, mlstm_norm_g, ret_norm_g, w_o, ln1_g, ln1_b, w_up, conv_w, conv_b, w_down, ln2_g, ln2_b):
    depth = w_in.shape[0]
    alpha = (2 * depth) ** 0.25
    row = lambda a: a.astype(F32).reshape(1, -1)
    layers = []
    for l in range(depth):
        layers.append(_layer_weights(l, w_in, b_if, w_o, w_up, w_down) + (
            ret_log_decay[l].astype(F32), row(mlstm_norm_g[l]), row(ret_norm_g[l]),
            (row(ln1_g[l]), row(ln1_b[l])), conv_w[l].astype(F32), row(conv_b[l]),
            (row(ln2_g[l]), row(ln2_b[l]))))
    ln_in = (row(ln_in_g), row(ln_in_b))
    return (_trunk(x_prompt, ln_in, layers, alpha), _trunk(x_sample, ln_in, layers, alpha))
```
